```python
import math
import jax, jax.numpy as jnp
from jax import lax
import numpy as np


D_MODEL = 1024
BATCH = 2
SEQ = 16384
DEPTH = 4

CHUNK = 64
QBLOCK = 128
MIX_WIDTH = D_MODEL
HEAD_DIM = 64
CONV_WIDTH = MIX_WIDTH // 4
CONV_K = 3
ATT_WIDTH = (MIX_WIDTH - CONV_WIDTH) // 2
DSA_HEADS = ATT_WIDTH // HEAD_DIM
CHK_HEADS = ATT_WIDTH // HEAD_DIM
ROPE_DIM = HEAD_DIM // 4
ROPE_THETA = 500000.0
IDX_HEADS = 8
IDX_DIM = 32
IDX_ROPE_DIM = IDX_DIM // 4
TOPK_MAX = 256
BAND_CHUNKS = 8
REL_CLIP = 128
REL_SIZE = (CHUNK - 1) + REL_CLIP + 1
MEM_LEN = 256
MEM_HEADS = 4
MEM_HEAD_DIM = D_MODEL // MEM_HEADS
D_FF = 4 * D_MODEL
LN_EPS = 1e-5
ALPHA = (2.0 * DEPTH) ** 0.25
BETA = (8.0 * DEPTH) ** -0.25
IN_SIZES = (CONV_WIDTH, CONV_WIDTH, CONV_WIDTH,
            DSA_HEADS * HEAD_DIM, HEAD_DIM, HEAD_DIM,
            IDX_HEADS * IDX_DIM, IDX_DIM, IDX_HEADS,
            CHK_HEADS * HEAD_DIM, CHK_HEADS * HEAD_DIM, CHK_HEADS * HEAD_DIM)
IN_WIDTH = sum(IN_SIZES)

kernel_name = 'hybrid_conv_dsa_chunkband_deepnorm'


def layer_norm(x, g, b):
    xf = x.astype(jnp.float32)
    mu = jnp.mean(xf, axis=-1, keepdims=True)
    var = jnp.mean(jnp.square(xf - mu), axis=-1, keepdims=True)
    y = (xf - mu) * lax.rsqrt(var + LN_EPS) * g.astype(jnp.float32) + b.astype(jnp.float32)
    return y.astype(x.dtype)


def partial_rope(x, positions, rot_dim):
    half = rot_dim // 2
    inv_freq = jnp.power(jnp.float32(ROPE_THETA), -jnp.arange(half, dtype=jnp.float32) / half)
    ang = positions.astype(jnp.float32)[..., None] * inv_freq
    cos = jnp.cos(ang)[:, :, None, :]
    sin = jnp.sin(ang)[:, :, None, :]
    x1 = x[..., :half].astype(jnp.float32)
    x2 = x[..., half:rot_dim].astype(jnp.float32)
    rot = jnp.concatenate([x1 * cos - x2 * sin, x2 * cos + x1 * sin], axis=-1)
    return jnp.concatenate([rot.astype(x.dtype), x[..., rot_dim:]], axis=-1)


def short_conv_mixer(gate_b, gate_c, h, conv_w):
    u = gate_c * h
    S = u.shape[1]
    u_pad = jnp.pad(u, ((0, 0), (CONV_K - 1, 0), (0, 0)))
    y = conv_w[0] * u_pad[:, 0:S]
    for j in range(1, CONV_K):
        y = y + conv_w[j] * u_pad[:, j:j + S]
    return gate_b * y


def dsa_attention(q, k, v, iq, ik, iw):
    B_, S, H, Dh = q.shape
    nb = S // QBLOCK
    topk = min(TOPK_MAX, S // 4)
    scale = Dh ** -0.5
    key_chunk = jnp.arange(S) // CHUNK
    iw = iw * (IDX_HEADS ** -0.5)

    def to_blocks(a):
        return jnp.moveaxis(a.reshape((B_, nb, QBLOCK) + a.shape[2:]), 1, 0)

    def one_block(args):
        bi, qb, iqb, iwb = args
        q_chunk = (bi * QBLOCK + jnp.arange(QBLOCK)) // CHUNK
        logits = jnp.einsum('bqhd,bsd->bqhs', iqb, ik)
        score = jnp.einsum('bqh,bqhs->bqs', iwb, jax.nn.relu(logits)).astype(jnp.float32)
        admissible = key_chunk[None, :] <= q_chunk[:, None]
        score = jnp.where(admissible[None], score, -jnp.inf)
        _, idx = lax.top_k(score, topk)
        valid = (idx // CHUNK) <= q_chunk[None, :, None]
        k_sel = jax.vmap(lambda a, ii: a[ii])(k, idx)
        v_sel = jax.vmap(lambda a, ii: a[ii])(v, idx)
        s = jnp.einsum('bqhd,bqkd->bhqk', qb, k_sel).astype(jnp.float32) * scale
        s = jnp.where(valid[:, None], s, -jnp.inf)
        p = jax.nn.softmax(s, axis=-1)
        return jnp.einsum('bhqk,bqkd->bqhd', p.astype(v_sel.dtype), v_sel)

    out = lax.map(one_block, (jnp.arange(nb), to_blocks(q), to_blocks(iq), to_blocks(iw)))
    return jnp.moveaxis(out, 0, 1).reshape(B_, S, H * Dh)


def chunk_band_attention(q, k, v, rel_bias):
    B_, S, H, Dh = q.shape
    nc = S // CHUNK
    band = (BAND_CHUNKS + 1) * CHUNK
    pad = BAND_CHUNKS * CHUNK
    scale = Dh ** -0.5
    k_pad = jnp.pad(k, ((0, 0), (pad, 0), (0, 0), (0, 0)))
    v_pad = jnp.pad(v, ((0, 0), (pad, 0), (0, 0), (0, 0)))
    i = jnp.arange(CHUNK)[:, None]
    j = jnp.arange(band)[None, :]
    rel = i - j + pad
    rel_idx = jnp.clip(rel, -(CHUNK - 1), REL_CLIP) + (CHUNK - 1)
    bias = rel_bias[:, rel_idx].astype(jnp.float32)
    q_chunks = jnp.moveaxis(q.reshape(B_, nc, CHUNK, H, Dh), 1, 0)

    def one_chunk(args):
        c, qc = args
        start = c * CHUNK
        kc = lax.dynamic_slice_in_dim(k_pad, start, band, axis=1)
        vc = lax.dynamic_slice_in_dim(v_pad, start, band, axis=1)
        s = jnp.einsum('bqhd,bkhd->bhqk', qc, kc).astype(jnp.float32) * scale + bias
        valid = jnp.arange(band) >= (pad - start)
        s = jnp.where(valid, s, -jnp.inf)
        p = jax.nn.softmax(s, axis=-1)
        return jnp.einsum('bhqk,bkhd->bqhd', p.astype(vc.dtype), vc)

    out = lax.map(one_chunk, (jnp.arange(nc), q_chunks))
    return jnp.moveaxis(out, 0, 1).reshape(B_, S, H * Dh)


def hybrid_mixer(x, positions, w_in, conv_w, rel_bias, w_out):
    B_, S, _ = x.shape
    proj = jnp.einsum('bsd,dp->bsp', x, w_in)
    offs = np.cumsum(np.array(IN_SIZES))[:-1].tolist()
    (cb, cc, ch, dq, dk, dv, iq, ik, iw, cq, ck, cv) = jnp.split(proj, offs, axis=-1)
    y_conv = short_conv_mixer(cb, cc, ch, conv_w)
    dq = partial_rope(dq.reshape(B_, S, DSA_HEADS, HEAD_DIM), positions, ROPE_DIM)
    dk = partial_rope(dk.reshape(B_, S, 1, HEAD_DIM), positions, ROPE_DIM)[:, :, 0]
    iq = partial_rope(iq.reshape(B_, S, IDX_HEADS, IDX_DIM), positions, IDX_ROPE_DIM)
    ik = partial_rope(ik.reshape(B_, S, 1, IDX_DIM), positions, IDX_ROPE_DIM)[:, :, 0]
    y_dsa = dsa_attention(dq, dk, dv, iq, ik, iw)
    cq = cq.reshape(B_, S, CHK_HEADS, HEAD_DIM)
    ck = ck.reshape(B_, S, CHK_HEADS, HEAD_DIM)
    cv = cv.reshape(B_, S, CHK_HEADS, HEAD_DIM)
    y_chk = chunk_band_attention(cq, ck, cv, rel_bias)
    y = jnp.concatenate([y_conv, y_dsa, y_chk], axis=-1)
    return jnp.einsum('bsm,md->bsd', y, w_out)


def memory_cross_attention(x, mem, w_q, w_kv, w_o):
    B_, S, _ = x.shape
    M = mem.shape[1]
    q = jnp.einsum('bsd,de->bse', x, w_q).reshape(B_, S, MEM_HEADS, MEM_HEAD_DIM)
    k, v = jnp.split(jnp.einsum('bmd,de->bme', mem, w_kv), 2, axis=-1)
    k = k.reshape(B_, M, MEM_HEADS, MEM_HEAD_DIM)
    v = v.reshape(B_, M, MEM_HEADS, MEM_HEAD_DIM)
    s = jnp.einsum('bshd,bmhd->bhsm', q, k).astype(jnp.float32) * (MEM_HEAD_DIM ** -0.5)
    p = jax.nn.softmax(s, axis=-1)
    o = jnp.einsum('bhsm,bmhd->bshd', p.astype(v.dtype), v).reshape(B_, S, D_MODEL)
    return jnp.einsum('bse,ed->bsd', o, w_o)


def squared_relu_mlp(x, w1, w2):
    h = jnp.square(jax.nn.relu(jnp.einsum('bsd,df->bsf', x, w1)))
    return jnp.einsum('bsf,fd->bsd', h, w2)


def setup_inputs(seed: int = 0) -> dict:
    key = jax.random.key(seed)
    ks = jax.random.split(key, 20)
    f32 = jnp.float32
    nrm = lambda k, shape, s: jax.random.normal(k, shape, f32) * s
    x = nrm(ks[0], (BATCH, SEQ, D_MODEL), 1.0)
    mem = nrm(ks[1], (BATCH, MEM_LEN, D_MODEL), 1.0)
    offset = jax.random.randint(ks[2], (BATCH,), 0, 4096, dtype=jnp.int32)
    positions = (offset[:, None] + jnp.arange(SEQ, dtype=jnp.int32)[None, :]).astype(jnp.int32)
    return {
        'x': x,
        'mem': mem,
        'positions': positions,
        'w_in': nrm(ks[3], (DEPTH, D_MODEL, IN_WIDTH), D_MODEL ** -0.5),
        'conv_w': nrm(ks[4], (DEPTH, CONV_K, CONV_WIDTH), CONV_K ** -0.5),
        'rel_bias': nrm(ks[5], (DEPTH, CHK_HEADS, REL_SIZE), 0.5),
        'w_mix_out': nrm(ks[6], (DEPTH, MIX_WIDTH, D_MODEL), BETA * MIX_WIDTH ** -0.5),
        'ln1_g': 1.0 + nrm(ks[7], (DEPTH, D_MODEL), 0.02),
        'ln1_b': nrm(ks[8], (DEPTH, D_MODEL), 0.02),
        'w_mq': nrm(ks[9], (DEPTH, D_MODEL, D_MODEL), D_MODEL ** -0.5),
        'w_mkv': nrm(ks[10], (DEPTH, D_MODEL, 2 * D_MODEL), D_MODEL ** -0.5),
        'w_mo': nrm(ks[11], (DEPTH, D_MODEL, D_MODEL), BETA * D_MODEL ** -0.5),
        'ln2_g': 1.0 + nrm(ks[12], (DEPTH, D_MODEL), 0.02),
        'ln2_b': nrm(ks[13], (DEPTH, D_MODEL), 0.02),
        'w_ff1': nrm(ks[14], (DEPTH, D_MODEL, D_FF), D_MODEL ** -0.5),
        'w_ff2': nrm(ks[15], (DEPTH, D_FF, D_MODEL), BETA * D_FF ** -0.5),
        'ln3_g': 1.0 + nrm(ks[16], (DEPTH, D_MODEL), 0.02),
        'ln3_b': nrm(ks[17], (DEPTH, D_MODEL), 0.02),
    }


def reference(x, mem, positions, w_in, conv_w, rel_bias, w_mix_out, ln1_g, ln1_b,
              w_mq, w_mkv, w_mo, ln2_g, ln2_b, w_ff1, w_ff2, ln3_g, ln3_b):
    for l in range(DEPTH):
        mix = hybrid_mixer(x, positions, w_in[l], conv_w[l], rel_bias[l], w_mix_out[l])
        x = layer_norm(ALPHA * x + mix, ln1_g[l], ln1_b[l])
        cross = memory_cross_attention(x, mem, w_mq[l], w_mkv[l], w_mo[l])
        x = layer_norm(ALPHA * x + cross, ln2_g[l], ln2_b[l])
        ff = squared_relu_mlp(x, w_ff1[l], w_ff2[l])
        x = layer_norm(ALPHA * x + ff, ln3_g[l], ln3_b[l])
    return x
```

```python
import functools

import numpy as np
import jax
import jax.numpy as jnp
from jax import lax
from jax.experimental import pallas as pl
from jax.experimental.pallas import tpu as pltpu

F32 = jnp.float32
BF16 = jnp.bfloat16
I32 = jnp.int32

D_MODEL = 1024
CHUNK = 64
HEAD_DIM = 64
CONV_WIDTH = 256
CONV_K = 3
ATT_WIDTH = 384
DSA_HEADS = 6
CHK_HEADS = 6
ROPE_DIM = 16
ROPE_THETA = 500000.0
IDX_HEADS = 8
IDX_DIM = 32
IDX_ROPE_DIM = 8
TOPK_MAX = 256
BAND_CHUNKS = 8
REL_CLIP = 128
MEM_HEADS = 4
MEM_HEAD_DIM = 256
D_FF = 4096
LN_EPS = 1e-5
DEPTH = 4
ALPHA = (2.0 * DEPTH) ** 0.25

LANES = 128
SUBLANES = 8
VMEM_LIMIT = 52 * 1024 * 1024

COL_CQ, COL_CK, COL_CV = 0, 384, 768
COL_DQ = 1152
COL_CONV = 1536
COL_IQ = 2304
COL_DKV = 2560
COL_IKW = 2688
PROJ_W = 2816

TM_PROJ = 512
TM_PREP = 512
TQ_DSA = 128
TK_DSA = 512
TQ_BAND = 256
TM_OUT = 512
TM_CROSS = 512
TM_MLP = 1024
TF_MLP = 512

NEG_BIG = -1e30
INT_MIN = -2147483648


def _cparams(sem):
    return pltpu.CompilerParams(dimension_semantics=sem, vmem_limit_bytes=VMEM_LIMIT)


def _layer_norm(y, g, b):
    mu = jnp.mean(y, axis=-1, keepdims=True)
    d = y - mu
    var = jnp.mean(d * d, axis=-1, keepdims=True)
    return d * lax.rsqrt(var + LN_EPS) * g + b


def _rope_table_kernel(pos_ref, freq_ref, mc_ref, m1_ref, m2_ref, c_ref, s1_ref, s2_ref):
    ang = pos_ref[0].astype(F32) * freq_ref[...]
    c = jnp.cos(ang)
    s = jnp.sin(ang)
    mc = mc_ref[...]
    c_ref[0] = c * mc + (1.0 - mc)
    s1_ref[0] = s * m1_ref[...]
    s2_ref[0] = -(s * m2_ref[...])


def _rope_lane_consts(head_dim, rot_dim):
    half = rot_dim // 2
    inv_freq = np.power(np.float32(ROPE_THETA), -np.arange(half, dtype=np.float32) / np.float32(half)).astype(np.float32)
    lane = np.arange(LANES)
    dd = lane % head_dim
    freq = np.where(dd < rot_dim, inv_freq[lane % half], 0.0).astype(np.float32)
    mc = (dd < rot_dim).astype(np.float32)
    m1 = ((dd >= half) & (dd < rot_dim)).astype(np.float32)
    m2 = (dd < half).astype(np.float32)
    return [jnp.asarray(a.reshape(1, LANES)) for a in (freq, mc, m1, m2)]


def rope_tables(positions, head_dim, rot_dim):
    B, S = positions.shape
    tm = min(1024, S)
    consts = _rope_lane_consts(head_dim, rot_dim)
    row = pl.BlockSpec((1, LANES), lambda b, i: (0, 0))
    tab = pl.BlockSpec((1, tm, LANES), lambda b, i: (b, i, 0))
    out = jax.ShapeDtypeStruct((B, S, LANES), F32)
    return pl.pallas_call(
        _rope_table_kernel,
        out_shape=(out, out, out),
        grid=(B, S // tm),
        in_specs=[pl.BlockSpec((1, tm, 1), lambda b, i: (b, i, 0)), row, row, row, row],
        out_specs=(tab, tab, tab),
        compiler_params=_cparams(("parallel", "parallel")),
    )(positions.reshape(B, S, 1), *consts)


def _mm_kernel(x_ref, w_ref, o_ref):
    o_ref[...] = jnp.dot(x_ref[...].astype(BF16), w_ref[...], preferred_element_type=F32)


def matmul(x, w, tm):
    M, K = x.shape
    N = w.shape[1]
    return pl.pallas_call(
        _mm_kernel,
        out_shape=jax.ShapeDtypeStruct((M, N), F32),
        grid=(M // tm,),
        in_specs=[pl.BlockSpec((tm, K), lambda i: (i, 0)), pl.BlockSpec((K, N), lambda i: (0, 0))],
        out_specs=pl.BlockSpec((tm, N), lambda i: (i, 0)),
        compiler_params=_cparams(("parallel",)),
    )(x, w)


def _rope_apply(x, c, s1, s2, half):
    return x * c + pltpu.roll(x, half, 1) * s1 + pltpu.roll(x, LANES - half, 1) * s2


def _prep_kernel(dq_ref, conv_ref, halo_ref, iq_ref, dkv_ref, ikw_ref,
                 cq_ref, s1q_ref, s2q_ref, ci_ref, s1i_ref, s2i_ref, cw_ref,
                 yconv_ref, q_ref, kv_ref, iqr_ref, ikwr_ref, u_scr):
    tm = dq_ref.shape[1]
    i = pl.program_id(1)
    cv = conv_ref[0]
    gate_b = cv[:, 0:CONV_WIDTH]
    u = cv[:, CONV_WIDTH:2 * CONV_WIDTH] * cv[:, 2 * CONV_WIDTH:3 * CONV_WIDTH]
    hv = halo_ref[0]
    hu = hv[:, CONV_WIDTH:2 * CONV_WIDTH] * hv[:, 2 * CONV_WIDTH:3 * CONV_WIDTH]
    hu = jnp.where(i > 0, hu, 0.0)
    u_scr[0:SUBLANES, :] = hu
    u_scr[SUBLANES:SUBLANES + tm, :] = u
    u1 = u_scr[SUBLANES - 1:SUBLANES - 1 + tm, :]
    u2 = u_scr[SUBLANES - 2:SUBLANES - 2 + tm, :]
    cw = cw_ref[...]
    y = cw[0:1, :] * u2 + cw[1:2, :] * u1 + cw[2:3, :] * u
    yconv_ref[0] = (gate_b * y).astype(BF16)
    cq, s1q, s2q = cq_ref[0], s1q_ref[0], s2q_ref[0]
    ci, s1i, s2i = ci_ref[0], s1i_ref[0], s2i_ref[0]
    hq = ROPE_DIM // 2
    hi = IDX_ROPE_DIM // 2
    dq = dq_ref[0]
    scale = HEAD_DIM ** -0.5
    for j in range(ATT_WIDTH // LANES):
        xt = dq[:, j * LANES:(j + 1) * LANES]
        q_ref[0, :, j * LANES:(j + 1) * LANES] = (_rope_apply(xt, cq, s1q, s2q, hq) * scale).astype(BF16)
    lane = lax.broadcasted_iota(I32, (1, LANES), 1)
    is_k = lane < HEAD_DIM
    kv_ref[0] = _rope_apply(dkv_ref[0], jnp.where(is_k, cq, 1.0), jnp.where(is_k, s1q, 0.0),
                            jnp.where(is_k, s2q, 0.0), hq).astype(BF16)
    iq = iq_ref[0]
    for j in range(IDX_HEADS * IDX_DIM // LANES):
        xt = iq[:, j * LANES:(j + 1) * LANES]
        iqr_ref[0, :, j * LANES:(j + 1) * LANES] = _rope_apply(xt, ci, s1i, s2i, hi)
    is_ik = lane < IDX_DIM
    is_iw = (lane >= IDX_DIM) & (lane < IDX_DIM + IDX_HEADS)
    ikw = _rope_apply(ikw_ref[0], jnp.where(is_ik, ci, 1.0), jnp.where(is_ik, s1i, 0.0),
                      jnp.where(is_ik, s2i, 0.0), hi)
    ikwr_ref[0] = ikw * jnp.where(is_iw, IDX_HEADS ** -0.5, 1.0)


def prep(proj, tabs_q, tabs_i, conv_w):
    B, S, _ = proj.shape
    tm = min(TM_PREP, S)
    hb = tm // SUBLANES

    def col(width, start):
        return pl.BlockSpec((1, tm, width), lambda b, i: (b, i, start // width))

    tab = pl.BlockSpec((1, tm, LANES), lambda b, i: (b, i, 0))
    halo = pl.BlockSpec((1, SUBLANES, 3 * CONV_WIDTH),
                        lambda b, i: (b, jnp.maximum(i * hb - 1, 0), COL_CONV // (3 * CONV_WIDTH)))

    def out(width, dtype):
        return jax.ShapeDtypeStruct((B, S, width), dtype), pl.BlockSpec((1, tm, width), lambda b, i: (b, i, 0))

    outs = [out(CONV_WIDTH, BF16), out(ATT_WIDTH, BF16), out(LANES, BF16),
            out(IDX_HEADS * IDX_DIM, F32), out(LANES, F32)]
    return pl.pallas_call(
        _prep_kernel,
        out_shape=tuple(o[0] for o in outs),
        grid=(B, S // tm),
        in_specs=[col(ATT_WIDTH, COL_DQ), col(3 * CONV_WIDTH, COL_CONV), halo,
                  col(IDX_HEADS * IDX_DIM, COL_IQ), col(LANES, COL_DKV), col(LANES, COL_IKW),
                  tab, tab, tab, tab, tab, tab,
                  pl.BlockSpec((CONV_K, CONV_WIDTH), lambda b, i: (0, 0))],
        out_specs=tuple(o[1] for o in outs),
        scratch_shapes=[pltpu.VMEM((tm + SUBLANES, CONV_WIDTH), F32)],
        compiler_params=_cparams(("parallel", "parallel")),
    )(proj, proj, proj, proj, proj, proj, *tabs_q, *tabs_i, conv_w)


def _sortable_key_np(v):
    b = np.array([v], dtype=np.float32).view(np.int32)[0]
    return int(b ^ ((b >> 31) & 0x7FFFFFFF))


KEY_LOWEST_FINITE = _sortable_key_np(-3.0e38)


def _dsa_kernel(iq3_ref, iw_ref, ik3_ref, q_ref, kt_ref, v_ref, o_ref,
                key_scr, iwb_scr, m_scr, l_scr, acc_scr, *, topk):
    tq = TQ_DSA
    tk = TK_DSA
    nsub = tk // LANES
    qb = pl.program_id(1)
    nk = (qb + 1) * tq
    ntiles = (nk + tk - 1) // tk
    iw = iw_ref[0]
    for h in range(IDX_HEADS):
        iwb_scr[h] = jnp.broadcast_to(iw[:, IDX_DIM + h:IDX_DIM + h + 1], (tq, LANES))
    row = lax.broadcasted_iota(I32, (tq, 1), 0)
    lim = ((qb * tq + row) // CHUNK + 1) * CHUNK
    lane = lax.broadcasted_iota(I32, (tq, LANES), 1)

    def score_tile(t, carry):
        lg = jnp.dot(iq3_ref[0, 0], ik3_ref[0, t], preferred_element_type=F32)
        for j in range(nsub):
            sc = jnp.zeros((tq, LANES), F32)
            for h in range(IDX_HEADS):
                sc = sc + iwb_scr[h] * jnp.maximum(lg[h * tq:(h + 1) * tq, j * LANES:(j + 1) * LANES], 0.0)
            kidx = t * tk + j * LANES + lane
            sc = jnp.where(kidx < lim, sc, -jnp.inf)
            bits = pltpu.bitcast(sc, I32)
            key_scr[t, :, j * LANES:(j + 1) * LANES] = bits ^ ((bits >> 31) & 0x7FFFFFFF)
        return carry

    lax.fori_loop(0, ntiles, score_tile, 0)

    def bit_step(i, t_u):
        cand = t_u | jnp.left_shift(jnp.int32(1), 31 - i)
        cand_b = jnp.broadcast_to(cand ^ INT_MIN, (tq, LANES))

        def count_tile(t, cnt):
            for j in range(nsub):
                cnt = cnt + (key_scr[t, :, j * LANES:(j + 1) * LANES] >= cand_b).astype(I32)
            return cnt

        cnt = lax.fori_loop(0, ntiles, count_tile, jnp.zeros((tq, LANES), I32))
        total = jnp.sum(cnt, axis=1, keepdims=True)
        return jnp.where(total >= topk, cand, t_u)

    t_u = lax.fori_loop(0, 32, bit_step, jnp.zeros((tq, 1), I32))
    thr = jnp.maximum(t_u ^ INT_MIN, KEY_LOWEST_FINITE)
    thr_b = jnp.broadcast_to(thr, (tq, tk))

    m_scr[...] = jnp.full(m_scr.shape, NEG_BIG, F32)
    l_scr[...] = jnp.zeros(l_scr.shape, F32)
    acc_scr[...] = jnp.zeros(acc_scr.shape, F32)

    def attn_tile(t, carry):
        s = jnp.dot(q_ref[0, 0], kt_ref[0, t], preferred_element_type=F32)
        sel = key_scr[t] >= thr_b
        s = jnp.where(sel[None], s.reshape(DSA_HEADS, tq, tk), NEG_BIG)
        m_old = m_scr[...]
        m_new = jnp.maximum(m_old, jnp.max(s, axis=-1, keepdims=True))
        a = jnp.exp(m_old - m_new)
        p = jnp.exp(s - m_new)
        l_scr[...] = a * l_scr[...] + jnp.sum(p, axis=-1, keepdims=True)
        pv = jnp.dot(p.astype(BF16).reshape(DSA_HEADS * tq, tk), v_ref[0, t], preferred_element_type=F32)
        acc_scr[...] = a * acc_scr[...] + pv.reshape(DSA_HEADS, tq, HEAD_DIM)
        m_scr[...] = m_new
        return carry

    lax.fori_loop(0, ntiles, attn_tile, 0)
    o = acc_scr[...] / l_scr[...]
    o_ref[0, 0] = o.reshape(DSA_HEADS * tq, HEAD_DIM).astype(BF16)


def dsa_attention(q, kv, iqr, ikw):
    B, S, _ = q.shape
    tq, tk = TQ_DSA, TK_DSA
    nqb, nkt = S // tq, S // tk
    topk = min(TOPK_MAX, S // 4)
    iq = iqr.reshape(B, nqb, tq, IDX_HEADS, IDX_DIM).transpose(0, 1, 3, 2, 4).reshape(B, nqb, IDX_HEADS * tq, IDX_DIM)
    iq_hi = iq.astype(BF16)
    iq_lo = (iq - iq_hi.astype(F32)).astype(BF16)
    iq3 = jnp.concatenate([iq_hi, iq_lo, iq_hi, jnp.zeros_like(iq_hi)], axis=-1)
    ik = ikw[..., :IDX_DIM]
    ik_hi = ik.astype(BF16)
    ik_lo = (ik - ik_hi.astype(F32)).astype(BF16)
    ik3 = jnp.concatenate([ik_hi, ik_hi, ik_lo, jnp.zeros_like(ik_hi)], axis=-1)
    ik3 = ik3.reshape(B, nkt, tk, LANES).transpose(0, 1, 3, 2)
    qst = q.reshape(B, nqb, tq, DSA_HEADS, HEAD_DIM).transpose(0, 1, 3, 2, 4).reshape(B, nqb, DSA_HEADS * tq, HEAD_DIM)
    kt = kv[..., :HEAD_DIM].reshape(B, nkt, tk, HEAD_DIM).transpose(0, 1, 3, 2)
    v = kv[..., HEAD_DIM:].reshape(B, nkt, tk, HEAD_DIM)
    out = pl.pallas_call(
        functools.partial(_dsa_kernel, topk=topk),
        out_shape=jax.ShapeDtypeStruct((B, nqb, DSA_HEADS * tq, HEAD_DIM), BF16),
        grid=(B, nqb),
        in_specs=[
            pl.BlockSpec((1, 1, IDX_HEADS * tq, LANES), lambda b, i: (b, i, 0, 0)),
            pl.BlockSpec((1, tq, LANES), lambda b, i: (b, i, 0)),
            pl.BlockSpec((1, nkt, LANES, tk), lambda b, i: (b, 0, 0, 0)),
            pl.BlockSpec((1, 1, DSA_HEADS * tq, HEAD_DIM), lambda b, i: (b, i, 0, 0)),
            pl.BlockSpec((1, nkt, HEAD_DIM, tk), lambda b, i: (b, 0, 0, 0)),
            pl.BlockSpec((1, nkt, tk, HEAD_DIM), lambda b, i: (b, 0, 0, 0)),
        ],
        out_specs=pl.BlockSpec((1, 1, DSA_HEADS * tq, HEAD_DIM), lambda b, i: (b, i, 0, 0)),
        scratch_shapes=[
            pltpu.VMEM((nkt, tq, tk), I32),
            pltpu.VMEM((IDX_HEADS, tq, LANES), F32),
            pltpu.VMEM((DSA_HEADS, tq, 1), F32),
            pltpu.VMEM((DSA_HEADS, tq, 1), F32),
            pltpu.VMEM((DSA_HEADS, tq, HEAD_DIM), F32),
        ],
        compiler_params=_cparams(("parallel", "arbitrary")),
    )(iq3, ikw, ik3, qst, kt, v)
    return out.reshape(B, nqb, DSA_HEADS, tq, HEAD_DIM).transpose(0, 1, 3, 2, 4).reshape(B, S, ATT_WIDTH)


def _band_kernel(q_ref, k0_ref, k1_ref, k2_ref, v0_ref, v1_ref, v2_ref, bias_ref, o_ref):
    tq = TQ_BAND
    i = pl.program_id(1)
    q = (q_ref[0] * (HEAD_DIM ** -0.5)).astype(BF16)
    ks = [r[0].astype(BF16) for r in (k0_ref, k1_ref, k2_ref)]
    vs = [r[0].astype(BF16) for r in (v0_ref, v1_ref, v2_ref)]
    nt = len(ks)
    for h in range(CHK_HEADS):
        hs = slice(h * HEAD_DIM, (h + 1) * HEAD_DIM)
        parts = []
        for j in range(nt):
            s = lax.dot_general(q[:, hs], ks[j][:, hs], (((1,), (1,)), ((), ())), preferred_element_type=F32)
            s = s + bias_ref[h, :, j * tq:(j + 1) * tq]
            if j < nt - 1:
                s = jnp.where(i + j >= nt - 1, s, NEG_BIG)
            parts.append(s)
        m = functools.reduce(jnp.maximum, [jnp.max(s, axis=-1, keepdims=True) for s in parts])
        ps = [jnp.exp(s - m) for s in parts]
        l = functools.reduce(jnp.add, [jnp.sum(p, axis=-1, keepdims=True) for p in ps])
        o = functools.reduce(jnp.add, [jnp.dot(p.astype(BF16), vs[j][:, hs], preferred_element_type=F32)
                                        for j, p in enumerate(ps)])
        o_ref[0, :, hs] = (o / l).astype(BF16)


def _band_bias(rel_bias):
    tq = TQ_BAND
    nt = BAND_CHUNKS * CHUNK // tq + 1
    win = nt * tq
    qi = np.arange(tq)[:, None]
    kj = np.arange(win)[None, :]
    a = qi // CHUNK
    kc = kj // CHUNK
    vis = (kc >= a) & (kc <= a + BAND_CHUNKS)
    jb = (kc - a) * CHUNK + kj % CHUNK
    rel = qi % CHUNK - jb + BAND_CHUNKS * CHUNK
    idx = np.clip(rel, -(CHUNK - 1), REL_CLIP) + (CHUNK - 1)
    return jnp.where(jnp.asarray(vis)[None], rel_bias[:, jnp.asarray(idx)], NEG_BIG).astype(F32)


def band_attention(proj, rel_bias):
    B, S, _ = proj.shape
    tq = TQ_BAND
    nt = BAND_CHUNKS * CHUNK // tq + 1
    bias = _band_bias(rel_bias)

    def kspec(col, j):
        return pl.BlockSpec((1, tq, ATT_WIDTH), lambda b, i: (b, jnp.maximum(i + j - (nt - 1), 0), col // ATT_WIDTH))

    return pl.pallas_call(
        _band_kernel,
        out_shape=jax.ShapeDtypeStruct((B, S, ATT_WIDTH), BF16),
        grid=(B, S // tq),
        in_specs=[pl.BlockSpec((1, tq, ATT_WIDTH), lambda b, i: (b, i, COL_CQ // ATT_WIDTH))]
                 + [kspec(COL_CK, j) for j in range(nt)] + [kspec(COL_CV, j) for j in range(nt)]
                 + [pl.BlockSpec((CHK_HEADS, tq, nt * tq), lambda b, i: (0, 0, 0))],
        out_specs=pl.BlockSpec((1, tq, ATT_WIDTH), lambda b, i: (b, i, 0)),
        compiler_params=_cparams(("parallel", "parallel")),
    )(proj, proj, proj, proj, proj, proj, proj, bias)


def _outproj_kernel(yc_ref, yd_ref, yb_ref, x_ref, w_ref, g_ref, b_ref, o_ref):
    c0, c1 = CONV_WIDTH, CONV_WIDTH + ATT_WIDTH
    mix = jnp.dot(yc_ref[...], w_ref[0:c0, :], preferred_element_type=F32)
    mix = mix + jnp.dot(yd_ref[...], w_ref[c0:c1, :], preferred_element_type=F32)
    mix = mix + jnp.dot(yb_ref[...], w_ref[c1:D_MODEL, :], preferred_element_type=F32)
    o_ref[...] = _layer_norm(ALPHA * x_ref[...] + mix, g_ref[...], b_ref[...])


def outproj_ln(yc, yd, yb, x, w, g, b):
    T = x.shape[0]
    tm = min(TM_OUT, T)
    rows = lambda width: pl.BlockSpec((tm, width), lambda i: (i, 0))
    vec = pl.BlockSpec((1, D_MODEL), lambda i: (0, 0))
    return pl.pallas_call(
        _outproj_kernel,
        out_shape=jax.ShapeDtypeStruct((T, D_MODEL), F32),
        grid=(T // tm,),
        in_specs=[rows(CONV_WIDTH), rows(ATT_WIDTH), rows(ATT_WIDTH), rows(D_MODEL),
                  pl.BlockSpec((D_MODEL, D_MODEL), lambda i: (0, 0)), vec, vec],
        out_specs=rows(D_MODEL),
        compiler_params=_cparams(("parallel",)),
    )(yc, yd, yb, x, w, g, b)


def _cross_kernel(x_ref, wq_ref, k_ref, v_ref, wo_ref, g_ref, b_ref, o_ref):
    x = x_ref[0]
    q = jnp.dot(x.astype(BF16), wq_ref[...], preferred_element_type=F32)
    q = (q * (MEM_HEAD_DIM ** -0.5)).astype(BF16)
    k = k_ref[0]
    v = v_ref[0]
    outs = []
    for h in range(MEM_HEADS):
        hs = slice(h * MEM_HEAD_DIM, (h + 1) * MEM_HEAD_DIM)
        s = lax.dot_general(q[:, hs], k[:, hs], (((1,), (1,)), ((), ())), preferred_element_type=F32)
        p = jnp.exp(s - jnp.max(s, axis=-1, keepdims=True))
        p = p / jnp.sum(p, axis=-1, keepdims=True)
        outs.append(jnp.dot(p.astype(BF16), v[:, hs], preferred_element_type=F32).astype(BF16))
    o = jnp.concatenate(outs, axis=-1)
    cross = jnp.dot(o, wo_ref[...], preferred_element_type=F32)
    o_ref[0] = _layer_norm(ALPHA * x + cross, g_ref[...], b_ref[...])


def cross_ln(x, k, v, wq, wo, g, b):
    B, S, _ = x.shape
    M = k.shape[1]
    tm = min(TM_CROSS, S)
    full = lambda shape: pl.BlockSpec(shape, lambda bb, i: (0,) * len(shape))
    return pl.pallas_call(
        _cross_kernel,
        out_shape=jax.ShapeDtypeStruct((B, S, D_MODEL), F32),
        grid=(B, S // tm),
        in_specs=[pl.BlockSpec((1, tm, D_MODEL), lambda bb, i: (bb, i, 0)),
                  full((D_MODEL, D_MODEL)),
                  pl.BlockSpec((1, M, D_MODEL), lambda bb, i: (bb, 0, 0)),
                  pl.BlockSpec((1, M, D_MODEL), lambda bb, i: (bb, 0, 0)),
                  full((D_MODEL, D_MODEL)), full((1, D_MODEL)), full((1, D_MODEL))],
        out_specs=pl.BlockSpec((1, tm, D_MODEL), lambda bb, i: (bb, i, 0)),
        compiler_params=_cparams(("parallel", "parallel")),
    )(x, wq, k, v, wo, g, b)


def _mlp_kernel(x_ref, w1_ref, w2_ref, g_ref, b_ref, o_ref, acc_ref):
    j = pl.program_id(1)

    @pl.when(j == 0)
    def _():
        acc_ref[...] = jnp.zeros(acc_ref.shape, F32)

    h = jnp.dot(x_ref[...].astype(BF16), w1_ref[...], preferred_element_type=F32)
    h = jnp.square(jnp.maximum(h, 0.0)).astype(BF16)
    acc_ref[...] += jnp.dot(h, w2_ref[...], preferred_element_type=F32)

    @pl.when(j == pl.num_programs(1) - 1)
    def _():
        o_ref[...] = _layer_norm(ALPHA * x_ref[...] + acc_ref[...], g_ref[...], b_ref[...])


def mlp_ln(x, w1, w2, g, b):
    T = x.shape[0]
    tm = min(TM_MLP, T)
    tf = TF_MLP
    vec = pl.BlockSpec((1, D_MODEL), lambda i, j: (0, 0))
    return pl.pallas_call(
        _mlp_kernel,
        out_shape=jax.ShapeDtypeStruct((T, D_MODEL), F32),
        grid=(T // tm, D_FF // tf),
        in_specs=[pl.BlockSpec((tm, D_MODEL), lambda i, j: (i, 0)),
                  pl.BlockSpec((D_MODEL, tf), lambda i, j: (0, j)),
                  pl.BlockSpec((tf, D_MODEL), lambda i, j: (j, 0)), vec, vec],
        out_specs=pl.BlockSpec((tm, D_MODEL), lambda i, j: (i, 0)),
        scratch_shapes=[pltpu.VMEM((tm, D_MODEL), F32)],
        compiler_params=_cparams(("parallel", "arbitrary")),
    )(x, w1, w2, g, b)


def _permute_w_in(w_in):
    sizes = (CONV_WIDTH, CONV_WIDTH, CONV_WIDTH, ATT_WIDTH, HEAD_DIM, HEAD_DIM,
             IDX_HEADS * IDX_DIM, IDX_DIM, IDX_HEADS, ATT_WIDTH, ATT_WIDTH, ATT_WIDTH)
    offs = np.concatenate([[0], np.cumsum(sizes)])
    seg = [w_in[:, offs[k]:offs[k + 1]] for k in range(len(sizes))]
    cb, cc, ch, dq, dk, dv, iq, ik, iw, cq, ck, cv = seg
    pad = jnp.zeros((w_in.shape[0], LANES - IDX_DIM - IDX_HEADS), w_in.dtype)
    return jnp.concatenate([cq, ck, cv, dq, cb, cc, ch, iq, dk, dv, ik, iw, pad], axis=1).astype(BF16)


@jax.jit
def kernel(x, mem, positions, w_in, conv_w, rel_bias, w_mix_out, ln1_g, ln1_b,
           w_mq, w_mkv, w_mo, ln2_g, ln2_b, w_ff1, w_ff2, ln3_g, ln3_b):
    B, S, D = x.shape
    T = B * S
    M = mem.shape[1]
    tabs_q = rope_tables(positions, HEAD_DIM, ROPE_DIM)
    tabs_i = rope_tables(positions, IDX_DIM, IDX_ROPE_DIM)
    mem2 = mem.reshape(B * M, D)
    xf = x.reshape(T, D)
    vec = lambda a: a.reshape(1, D)
    for l in range(w_in.shape[0]):
        proj = matmul(xf, _permute_w_in(w_in[l]), min(TM_PROJ, T)).reshape(B, S, PROJ_W)
        yconv, q, kv, iqr, ikw = prep(proj, tabs_q, tabs_i, conv_w[l])
        ydsa = dsa_attention(q, kv, iqr, ikw)
        yband = band_attention(proj, rel_bias[l])
        xf = outproj_ln(yconv.reshape(T, -1), ydsa.reshape(T, -1), yband.reshape(T, -1), xf,
                        w_mix_out[l].astype(BF16), vec(ln1_g[l]), vec(ln1_b[l]))
        kvm = matmul(mem2, w_mkv[l].astype(BF16), min(512, B * M)).reshape(B, M, 2 * D)
        xf = cross_ln(xf.reshape(B, S, D), kvm[..., :D].astype(BF16), kvm[..., D:].astype(BF16),
                      w_mq[l].astype(BF16), w_mo[l].astype(BF16), vec(ln2_g[l]), vec(ln2_b[l])).reshape(T, D)
        xf = mlp_ln(xf, w_ff1[l].astype(BF16), w_ff2[l].astype(BF16), vec(ln3_g[l]), vec(ln3_b[l]))
    return xf.reshape(B, S, D)
```

```python
import functools

import numpy as np
import jax
import jax.numpy as jnp
from jax import lax
from jax.experimental import pallas as pl
from jax.experimental.pallas import tpu as pltpu

F32 = jnp.float32
BF16 = jnp.bfloat16
I32 = jnp.int32

D_MODEL = 1024
CHUNK = 64
HEAD_DIM = 64
CONV_WIDTH = 256
CONV_K = 3
ATT_WIDTH = 384
DSA_HEADS = 6
CHK_HEADS = 6
ROPE_DIM = 16
ROPE_THETA = 500000.0
IDX_HEADS = 8
IDX_DIM = 32
IDX_ROPE_DIM = 8
TOPK_MAX = 256
BAND_CHUNKS = 8
REL_CLIP = 128
MEM_HEADS = 4
MEM_HEAD_DIM = 256
D_FF = 4096
LN_EPS = 1e-5
DEPTH = 4
ALPHA = (2.0 * DEPTH) ** 0.25

LANES = 128
SUBLANES = 8
VMEM_LIMIT = 52 * 1024 * 1024

COL_CQ, COL_CK, COL_CV = 0, 384, 768
COL_DQ = 1152
COL_CONV = 1536
COL_IQ = 2304
COL_DKV = 2560
COL_IKW = 2688
PROJ_W = 2816

TM_PROJ = 512
TM_PREP = 512
TQ_DSA = 128
TK_DSA = 512
TKA_DSA = 1024
TQ_BAND = 256
TM_OUT = 512
TM_CROSS = 512
TM_MLP = 1024
TF_MLP = 512

LOG2_E = 1.4426950408889634
NEG_BIG = -1e30
INT_MIN = -2147483648


def _cparams(sem):
    return pltpu.CompilerParams(dimension_semantics=sem, vmem_limit_bytes=VMEM_LIMIT)


def _layer_norm(y, g, b):
    mu = jnp.mean(y, axis=-1, keepdims=True)
    d = y - mu
    var = jnp.mean(d * d, axis=-1, keepdims=True)
    return d * lax.rsqrt(var + LN_EPS) * g + b


def _rope_table_kernel(pos_ref, freq_ref, mc_ref, m1_ref, m2_ref, c_ref, s1_ref, s2_ref):
    ang = pos_ref[0].astype(F32) * freq_ref[...]
    c = jnp.cos(ang)
    s = jnp.sin(ang)
    mc = mc_ref[...]
    c_ref[0] = c * mc + (1.0 - mc)
    s1_ref[0] = s * m1_ref[...]
    s2_ref[0] = -(s * m2_ref[...])


def _rope_lane_consts(head_dim, rot_dim):
    half = rot_dim // 2
    inv_freq = np.power(np.float32(ROPE_THETA), -np.arange(half, dtype=np.float32) / np.float32(half)).astype(np.float32)
    lane = np.arange(LANES)
    dd = lane % head_dim
    freq = np.where(dd < rot_dim, inv_freq[lane % half], 0.0).astype(np.float32)
    mc = (dd < rot_dim).astype(np.float32)
    m1 = ((dd >= half) & (dd < rot_dim)).astype(np.float32)
    m2 = (dd < half).astype(np.float32)
    return [jnp.asarray(a.reshape(1, LANES)) for a in (freq, mc, m1, m2)]


def rope_tables(positions, head_dim, rot_dim):
    B, S = positions.shape
    tm = min(1024, S)
    consts = _rope_lane_consts(head_dim, rot_dim)
    row = pl.BlockSpec((1, LANES), lambda b, i: (0, 0))
    tab = pl.BlockSpec((1, tm, LANES), lambda b, i: (b, i, 0))
    out = jax.ShapeDtypeStruct((B, S, LANES), F32)
    return pl.pallas_call(
        _rope_table_kernel,
        out_shape=(out, out, out),
        grid=(B, S // tm),
        in_specs=[pl.BlockSpec((1, tm, 1), lambda b, i: (b, i, 0)), row, row, row, row],
        out_specs=(tab, tab, tab),
        compiler_params=_cparams(("parallel", "parallel")),
    )(positions.reshape(B, S, 1), *consts)


def _mm_kernel(x_ref, w_ref, o_ref):
    o_ref[...] = jnp.dot(x_ref[...].astype(BF16), w_ref[...], preferred_element_type=F32)


def matmul(x, w, tm):
    M, K = x.shape
    N = w.shape[1]
    return pl.pallas_call(
        _mm_kernel,
        out_shape=jax.ShapeDtypeStruct((M, N), F32),
        grid=(M // tm,),
        in_specs=[pl.BlockSpec((tm, K), lambda i: (i, 0)), pl.BlockSpec((K, N), lambda i: (0, 0))],
        out_specs=pl.BlockSpec((tm, N), lambda i: (i, 0)),
        compiler_params=_cparams(("parallel",)),
    )(x, w)


def _rope_apply(x, c, s1, s2, half):
    return x * c + pltpu.roll(x, half, 1) * s1 + pltpu.roll(x, LANES - half, 1) * s2


def _prep_kernel(dq_ref, conv_ref, halo_ref, iq_ref, dkv_ref, ikw_ref,
                 cq_ref, s1q_ref, s2q_ref, ci_ref, s1i_ref, s2i_ref, cw_ref,
                 yconv_ref, q_ref, kv_ref, iqr_ref, ikwr_ref, u_scr):
    tm = dq_ref.shape[1]
    i = pl.program_id(1)
    cv = conv_ref[0]
    gate_b = cv[:, 0:CONV_WIDTH]
    u = cv[:, CONV_WIDTH:2 * CONV_WIDTH] * cv[:, 2 * CONV_WIDTH:3 * CONV_WIDTH]
    hv = halo_ref[0]
    hu = hv[:, CONV_WIDTH:2 * CONV_WIDTH] * hv[:, 2 * CONV_WIDTH:3 * CONV_WIDTH]
    hu = jnp.where(i > 0, hu, 0.0)
    u_scr[0:SUBLANES, :] = hu
    u_scr[SUBLANES:SUBLANES + tm, :] = u
    u1 = u_scr[SUBLANES - 1:SUBLANES - 1 + tm, :]
    u2 = u_scr[SUBLANES - 2:SUBLANES - 2 + tm, :]
    cw = cw_ref[...]
    y = cw[0:1, :] * u2 + cw[1:2, :] * u1 + cw[2:3, :] * u
    yconv_ref[0] = (gate_b * y).astype(BF16)
    cq, s1q, s2q = cq_ref[0], s1q_ref[0], s2q_ref[0]
    ci, s1i, s2i = ci_ref[0], s1i_ref[0], s2i_ref[0]
    hq = ROPE_DIM // 2
    hi = IDX_ROPE_DIM // 2
    dq = dq_ref[0]
    scale = HEAD_DIM ** -0.5 * LOG2_E
    for j in range(ATT_WIDTH // LANES):
        xt = dq[:, j * LANES:(j + 1) * LANES]
        q_ref[0, :, j * LANES:(j + 1) * LANES] = (_rope_apply(xt, cq, s1q, s2q, hq) * scale).astype(BF16)
    lane = lax.broadcasted_iota(I32, (1, LANES), 1)
    is_k = lane < HEAD_DIM
    kv_ref[0] = _rope_apply(dkv_ref[0], jnp.where(is_k, cq, 1.0), jnp.where(is_k, s1q, 0.0),
                            jnp.where(is_k, s2q, 0.0), hq).astype(BF16)
    iq = iq_ref[0]
    for j in range(IDX_HEADS * IDX_DIM // LANES):
        xt = iq[:, j * LANES:(j + 1) * LANES]
        iqr_ref[0, :, j * LANES:(j + 1) * LANES] = _rope_apply(xt, ci, s1i, s2i, hi)
    is_ik = lane < IDX_DIM
    is_iw = (lane >= IDX_DIM) & (lane < IDX_DIM + IDX_HEADS)
    ikw = _rope_apply(ikw_ref[0], jnp.where(is_ik, ci, 1.0), jnp.where(is_ik, s1i, 0.0),
                      jnp.where(is_ik, s2i, 0.0), hi)
    ikwr_ref[0] = ikw * jnp.where(is_iw, IDX_HEADS ** -0.5, 1.0)


def prep(proj, tabs_q, tabs_i, conv_w):
    B, S, _ = proj.shape
    tm = min(TM_PREP, S)
    hb = tm // SUBLANES

    def col(width, start):
        return pl.BlockSpec((1, tm, width), lambda b, i: (b, i, start // width))

    tab = pl.BlockSpec((1, tm, LANES), lambda b, i: (b, i, 0))
    halo = pl.BlockSpec((1, SUBLANES, 3 * CONV_WIDTH),
                        lambda b, i: (b, jnp.maximum(i * hb - 1, 0), COL_CONV // (3 * CONV_WIDTH)))

    def out(width, dtype):
        return jax.ShapeDtypeStruct((B, S, width), dtype), pl.BlockSpec((1, tm, width), lambda b, i: (b, i, 0))

    outs = [out(CONV_WIDTH, BF16), out(ATT_WIDTH, BF16), out(LANES, BF16),
            out(IDX_HEADS * IDX_DIM, F32), out(LANES, F32)]
    return pl.pallas_call(
        _prep_kernel,
        out_shape=tuple(o[0] for o in outs),
        grid=(B, S // tm),
        in_specs=[col(ATT_WIDTH, COL_DQ), col(3 * CONV_WIDTH, COL_CONV), halo,
                  col(IDX_HEADS * IDX_DIM, COL_IQ), col(LANES, COL_DKV), col(LANES, COL_IKW),
                  tab, tab, tab, tab, tab, tab,
                  pl.BlockSpec((CONV_K, CONV_WIDTH), lambda b, i: (0, 0))],
        out_specs=tuple(o[1] for o in outs),
        scratch_shapes=[pltpu.VMEM((tm + SUBLANES, CONV_WIDTH), F32)],
        compiler_params=_cparams(("parallel", "parallel")),
    )(proj, proj, proj, proj, proj, proj, *tabs_q, *tabs_i, conv_w)


def _sortable_key_np(v):
    b = np.array([v], dtype=np.float32).view(np.int32)[0]
    return int(b ^ ((b >> 31) & 0x7FFFFFFF))


KEY_LOWEST_FINITE = _sortable_key_np(-3.0e38)


def _dsa_kernel(iq3_ref, iw_ref, ik3_ref, q_ref, kt_ref, v_ref, o_ref,
                key_scr, iwb_scr, cmax_scr, bias_scr, p_scr, m_scr, l_scr, a_scr, acc_scr, *, topk):
    tq = TQ_DSA
    tk = TK_DSA
    tka = TKA_DSA
    nsub = tk // LANES
    nsuba = tka // LANES
    qb = pl.program_id(1)
    nk = (qb + 1) * tq
    ntiles = (nk + tk - 1) // tk
    iw = iw_ref[0]
    for h in range(IDX_HEADS):
        iwb_scr[h] = jnp.broadcast_to(iw[:, IDX_DIM + h:IDX_DIM + h + 1], (tq, LANES))
    cmax_scr[...] = jnp.full(cmax_scr.shape, -jnp.inf, F32)

    def score_tile(t, masked):
        lg = jnp.dot(iq3_ref[0, 0], ik3_ref[0, t], preferred_element_type=F32)
        if masked:
            row = lax.broadcasted_iota(I32, (tq, 1), 0)
            lim = ((qb * tq + row) // CHUNK + 1) * CHUNK
            lane = lax.broadcasted_iota(I32, (tq, LANES), 1)
        for j in range(nsub):
            sc = iwb_scr[0] * jnp.maximum(lg[0:tq, j * LANES:(j + 1) * LANES], 0.0)
            for h in range(1, IDX_HEADS):
                sc = sc + iwb_scr[h] * jnp.maximum(lg[h * tq:(h + 1) * tq, j * LANES:(j + 1) * LANES], 0.0)
            if masked:
                sc = jnp.where(t * tk + j * LANES + lane < lim, sc, -jnp.inf)
            cmax_scr[j % 2] = jnp.maximum(cmax_scr[j % 2], sc)
            bits = pltpu.bitcast(sc, I32)
            key_scr[t, :, j * LANES:(j + 1) * LANES] = bits ^ ((bits >> 31) & 0x7FFFFFFF)

    def score_body(t, carry):
        score_tile(t, False)
        return carry

    lax.fori_loop(0, ntiles - 1, score_body, 0)
    score_tile(ntiles - 1, True)

    @pl.when(ntiles % 2 == 1)
    def _():
        key_scr[ntiles] = jnp.full((tq, tk), INT_MIN, I32)

    def to_key(x):
        bits = pltpu.bitcast(x, I32)
        return bits ^ ((bits >> 31) & 0x7FFFFFFF)

    cm = cmax_scr[...]
    lo0 = to_key(jnp.min(jnp.minimum(cm[0], cm[1]), axis=1, keepdims=True))
    hi0 = to_key(jnp.max(jnp.maximum(cm[0], cm[1]), axis=1, keepdims=True))

    def bis_cond(c):
        lo, hi = c
        return jnp.max((hi > lo).astype(I32)) > 0

    def bis_body(c):
        lo, hi = c
        mid = (lo | hi) - ((lo ^ hi) >> 1)
        mid_b = jnp.broadcast_to(mid, (tq, LANES))

        def count_tile(t, cnt):
            for j in range(nsub):
                cnt = cnt + (key_scr[t, :, j * LANES:(j + 1) * LANES] >= mid_b).astype(I32)
            return cnt

        cnt = lax.fori_loop(0, ntiles, count_tile, jnp.zeros((tq, LANES), I32))
        total = jnp.sum(cnt, axis=1, keepdims=True)
        ge = total >= topk
        lo_n = jnp.where(ge, mid, lo)
        hi_n = jnp.where(total == topk, mid, jnp.where(ge, hi, mid - 1))
        return lo_n, hi_n

    lo, _ = lax.while_loop(bis_cond, bis_body, (lo0, hi0))
    thr = jnp.maximum(lo, KEY_LOWEST_FINITE)
    thr_b = jnp.broadcast_to(thr, (tq, LANES))

    m_scr[...] = jnp.full(m_scr.shape, NEG_BIG, F32)
    l_scr[...] = jnp.zeros(l_scr.shape, F32)
    acc_scr[...] = jnp.zeros(acc_scr.shape, F32)
    ratio = tka // tk

    def attn_tile(t, carry):
        s = jnp.dot(q_ref[0, 0], kt_ref[0, t], preferred_element_type=F32)
        for r in range(ratio):
            for j in range(nsub):
                sel = key_scr[ratio * t + r, :, j * LANES:(j + 1) * LANES] >= thr_b
                c0 = (r * nsub + j) * LANES
                bias_scr[:, c0:c0 + LANES] = jnp.where(sel, 0.0, NEG_BIG)
        for h in range(DSA_HEADS):
            r0 = h * tq
            sb = [s[r0:r0 + tq, j * LANES:(j + 1) * LANES] + bias_scr[:, j * LANES:(j + 1) * LANES]
                  for j in range(nsuba)]
            m_old = m_scr[h]
            m_new = jnp.maximum(m_old, jnp.max(functools.reduce(jnp.maximum, sb), axis=1, keepdims=True))
            a = jnp.exp2(m_old - m_new)
            ps = [jnp.exp2(x - m_new) for x in sb]
            l_scr[h] = a * l_scr[h] + functools.reduce(jnp.add, ps)
            m_scr[h] = m_new
            a_scr[h] = a
            for j in range(nsuba):
                p_scr[r0:r0 + tq, j * LANES:(j + 1) * LANES] = ps[j].astype(BF16)
        pv = jnp.dot(p_scr[...], v_ref[0, t], preferred_element_type=F32)
        for h in range(DSA_HEADS):
            acc_scr[h] = a_scr[h][:, 0:HEAD_DIM] * acc_scr[h] + pv[h * tq:(h + 1) * tq, :]
        return carry

    lax.fori_loop(0, (ntiles + ratio - 1) // ratio, attn_tile, 0)
    for h in range(DSA_HEADS):
        l = jnp.sum(l_scr[h], axis=1, keepdims=True)
        o_ref[0, 0, h * tq:(h + 1) * tq, :] = (acc_scr[h] / l).astype(BF16)


def dsa_attention(q, kv, iqr, ikw):
    B, S, _ = q.shape
    tq, tk = TQ_DSA, TK_DSA
    nqb, nkt = S // tq, S // tk
    topk = min(TOPK_MAX, S // 4)
    iq = iqr.reshape(B, nqb, tq, IDX_HEADS, IDX_DIM).transpose(0, 1, 3, 2, 4).reshape(B, nqb, IDX_HEADS * tq, IDX_DIM)
    iq_hi = iq.astype(BF16)
    iq_lo = (iq - iq_hi.astype(F32)).astype(BF16)
    iq3 = jnp.concatenate([iq_hi, iq_lo, iq_hi, jnp.zeros_like(iq_hi)], axis=-1)
    ik = ikw[..., :IDX_DIM]
    ik_hi = ik.astype(BF16)
    ik_lo = (ik - ik_hi.astype(F32)).astype(BF16)
    ik3 = jnp.concatenate([ik_hi, ik_hi, ik_lo, jnp.zeros_like(ik_hi)], axis=-1)
    ik3 = ik3.reshape(B, nkt, tk, LANES).transpose(0, 1, 3, 2)
    qst = q.reshape(B, nqb, tq, DSA_HEADS, HEAD_DIM).transpose(0, 1, 3, 2, 4).reshape(B, nqb, DSA_HEADS * tq, HEAD_DIM)
    tka = TKA_DSA
    nka = S // tka
    kt = kv[..., :HEAD_DIM].reshape(B, nka, tka, HEAD_DIM).transpose(0, 1, 3, 2)
    v = kv[..., HEAD_DIM:].reshape(B, nka, tka, HEAD_DIM)
    out = pl.pallas_call(
        functools.partial(_dsa_kernel, topk=topk),
        out_shape=jax.ShapeDtypeStruct((B, nqb, DSA_HEADS * tq, HEAD_DIM), BF16),
        grid=(B, nqb),
        in_specs=[
            pl.BlockSpec((1, 1, IDX_HEADS * tq, LANES), lambda b, i: (b, i, 0, 0)),
            pl.BlockSpec((1, tq, LANES), lambda b, i: (b, i, 0)),
            pl.BlockSpec((1, nkt, LANES, tk), lambda b, i: (b, 0, 0, 0)),
            pl.BlockSpec((1, 1, DSA_HEADS * tq, HEAD_DIM), lambda b, i: (b, i, 0, 0)),
            pl.BlockSpec((1, nka, HEAD_DIM, tka), lambda b, i: (b, 0, 0, 0)),
            pl.BlockSpec((1, nka, tka, HEAD_DIM), lambda b, i: (b, 0, 0, 0)),
        ],
        out_specs=pl.BlockSpec((1, 1, DSA_HEADS * tq, HEAD_DIM), lambda b, i: (b, i, 0, 0)),
        scratch_shapes=[
            pltpu.VMEM((nkt, tq, tk), I32),
            pltpu.VMEM((IDX_HEADS, tq, LANES), F32),
            pltpu.VMEM((2, tq, LANES), F32),
            pltpu.VMEM((tq, tka), F32),
            pltpu.VMEM((DSA_HEADS * tq, tka), BF16),
            pltpu.VMEM((DSA_HEADS, tq, LANES), F32),
            pltpu.VMEM((DSA_HEADS, tq, LANES), F32),
            pltpu.VMEM((DSA_HEADS, tq, LANES), F32),
            pltpu.VMEM((DSA_HEADS, tq, HEAD_DIM), F32),
        ],
        compiler_params=_cparams(("parallel", "arbitrary")),
    )(iq3, ikw, ik3, qst, kt, v)
    return out.reshape(B, nqb, DSA_HEADS, tq, HEAD_DIM).transpose(0, 1, 3, 2, 4).reshape(B, S, ATT_WIDTH)


def _band_kernel(q_ref, k0_ref, k1_ref, k2_ref, v0_ref, v1_ref, v2_ref, bias_ref, o_ref):
    tq = TQ_BAND
    i = pl.program_id(1)
    q = (q_ref[0] * (HEAD_DIM ** -0.5)).astype(BF16)
    ks = [r[0].astype(BF16) for r in (k0_ref, k1_ref, k2_ref)]
    vs = [r[0].astype(BF16) for r in (v0_ref, v1_ref, v2_ref)]
    nt = len(ks)
    for h in range(CHK_HEADS):
        hs = slice(h * HEAD_DIM, (h + 1) * HEAD_DIM)
        parts = []
        for j in range(nt):
            s = lax.dot_general(q[:, hs], ks[j][:, hs], (((1,), (1,)), ((), ())), preferred_element_type=F32)
            s = s + bias_ref[h, :, j * tq:(j + 1) * tq]
            if j < nt - 1:
                s = jnp.where(i + j >= nt - 1, s, NEG_BIG)
            parts.append(s)
        m = functools.reduce(jnp.maximum, [jnp.max(s, axis=-1, keepdims=True) for s in parts])
        ps = [jnp.exp(s - m) for s in parts]
        l = functools.reduce(jnp.add, [jnp.sum(p, axis=-1, keepdims=True) for p in ps])
        o = functools.reduce(jnp.add, [jnp.dot(p.astype(BF16), vs[j][:, hs], preferred_element_type=F32)
                                        for j, p in enumerate(ps)])
        o_ref[0, :, hs] = (o / l).astype(BF16)


def _band_bias(rel_bias):
    tq = TQ_BAND
    nt = BAND_CHUNKS * CHUNK // tq + 1
    win = nt * tq
    band = (BAND_CHUNKS + 1) * CHUNK
    H = rel_bias.shape[0]
    n = band + CHUNK - 1
    g = jnp.concatenate([rel_bias, jnp.broadcast_to(rel_bias[:, -1:], (H, n - rel_bias.shape[1]))], axis=1)[:, ::-1]
    hankel = jnp.tile(g, (1, CHUNK + 1))[:, :CHUNK * (n + 1)].reshape(H, CHUNK, n + 1)[:, :, :band]
    w = hankel[:, ::-1, :]
    rows = [jnp.pad(w, ((0, 0), (0, 0), (a * CHUNK, win - band - a * CHUNK)), constant_values=NEG_BIG)
            for a in range(tq // CHUNK)]
    return jnp.concatenate(rows, axis=1).astype(F32)


def band_attention(proj, rel_bias):
    B, S, _ = proj.shape
    tq = TQ_BAND
    nt = BAND_CHUNKS * CHUNK // tq + 1
    bias = _band_bias(rel_bias)

    def kspec(col, j):
        return pl.BlockSpec((1, tq, ATT_WIDTH), lambda b, i: (b, jnp.maximum(i + j - (nt - 1), 0), col // ATT_WIDTH))

    return pl.pallas_call(
        _band_kernel,
        out_shape=jax.ShapeDtypeStruct((B, S, ATT_WIDTH), BF16),
        grid=(B, S // tq),
        in_specs=[pl.BlockSpec((1, tq, ATT_WIDTH), lambda b, i: (b, i, COL_CQ // ATT_WIDTH))]
                 + [kspec(COL_CK, j) for j in range(nt)] + [kspec(COL_CV, j) for j in range(nt)]
                 + [pl.BlockSpec((CHK_HEADS, tq, nt * tq), lambda b, i: (0, 0, 0))],
        out_specs=pl.BlockSpec((1, tq, ATT_WIDTH), lambda b, i: (b, i, 0)),
        compiler_params=_cparams(("parallel", "parallel")),
    )(proj, proj, proj, proj, proj, proj, proj, bias)


def _outproj_kernel(yc_ref, yd_ref, yb_ref, x_ref, w_ref, g_ref, b_ref, o_ref):
    c0, c1 = CONV_WIDTH, CONV_WIDTH + ATT_WIDTH
    mix = jnp.dot(yc_ref[...], w_ref[0:c0, :], preferred_element_type=F32)
    mix = mix + jnp.dot(yd_ref[...], w_ref[c0:c1, :], preferred_element_type=F32)
    mix = mix + jnp.dot(yb_ref[...], w_ref[c1:D_MODEL, :], preferred_element_type=F32)
    o_ref[...] = _layer_norm(ALPHA * x_ref[...] + mix, g_ref[...], b_ref[...])


def outproj_ln(yc, yd, yb, x, w, g, b):
    T = x.shape[0]
    tm = min(TM_OUT, T)
    rows = lambda width: pl.BlockSpec((tm, width), lambda i: (i, 0))
    vec = pl.BlockSpec((1, D_MODEL), lambda i: (0, 0))
    return pl.pallas_call(
        _outproj_kernel,
        out_shape=jax.ShapeDtypeStruct((T, D_MODEL), F32),
        grid=(T // tm,),
        in_specs=[rows(CONV_WIDTH), rows(ATT_WIDTH), rows(ATT_WIDTH), rows(D_MODEL),
                  pl.BlockSpec((D_MODEL, D_MODEL), lambda i: (0, 0)), vec, vec],
        out_specs=rows(D_MODEL),
        compiler_params=_cparams(("parallel",)),
    )(yc, yd, yb, x, w, g, b)


def _cross_kernel(x_ref, wq_ref, k_ref, v_ref, wo_ref, g_ref, b_ref, o_ref):
    x = x_ref[0]
    q = jnp.dot(x.astype(BF16), wq_ref[...], preferred_element_type=F32)
    q = (q * (MEM_HEAD_DIM ** -0.5)).astype(BF16)
    k = k_ref[0]
    v = v_ref[0]
    outs = []
    for h in range(MEM_HEADS):
        hs = slice(h * MEM_HEAD_DIM, (h + 1) * MEM_HEAD_DIM)
        s = lax.dot_general(q[:, hs], k[:, hs], (((1,), (1,)), ((), ())), preferred_element_type=F32)
        p = jnp.exp(s - jnp.max(s, axis=-1, keepdims=True))
        p = p / jnp.sum(p, axis=-1, keepdims=True)
        outs.append(jnp.dot(p.astype(BF16), v[:, hs], preferred_element_type=F32).astype(BF16))
    o = jnp.concatenate(outs, axis=-1)
    cross = jnp.dot(o, wo_ref[...], preferred_element_type=F32)
    o_ref[0] = _layer_norm(ALPHA * x + cross, g_ref[...], b_ref[...])


def cross_ln(x, k, v, wq, wo, g, b):
    B, S, _ = x.shape
    M = k.shape[1]
    tm = min(TM_CROSS, S)
    full = lambda shape: pl.BlockSpec(shape, lambda bb, i: (0,) * len(shape))
    return pl.pallas_call(
        _cross_kernel,
        out_shape=jax.ShapeDtypeStruct((B, S, D_MODEL), F32),
        grid=(B, S // tm),
        in_specs=[pl.BlockSpec((1, tm, D_MODEL), lambda bb, i: (bb, i, 0)),
                  full((D_MODEL, D_MODEL)),
                  pl.BlockSpec((1, M, D_MODEL), lambda bb, i: (bb, 0, 0)),
                  pl.BlockSpec((1, M, D_MODEL), lambda bb, i: (bb, 0, 0)),
                  full((D_MODEL, D_MODEL)), full((1, D_MODEL)), full((1, D_MODEL))],
        out_specs=pl.BlockSpec((1, tm, D_MODEL), lambda bb, i: (bb, i, 0)),
        compiler_params=_cparams(("parallel", "parallel")),
    )(x, wq, k, v, wo, g, b)


def _mlp_kernel(x_ref, w1_ref, w2_ref, g_ref, b_ref, o_ref, acc_ref):
    j = pl.program_id(1)

    @pl.when(j == 0)
    def _():
        acc_ref[...] = jnp.zeros(acc_ref.shape, F32)

    h = jnp.dot(x_ref[...].astype(BF16), w1_ref[...], preferred_element_type=F32)
    h = jnp.square(jnp.maximum(h, 0.0)).astype(BF16)
    acc_ref[...] += jnp.dot(h, w2_ref[...], preferred_element_type=F32)

    @pl.when(j == pl.num_programs(1) - 1)
    def _():
        o_ref[...] = _layer_norm(ALPHA * x_ref[...] + acc_ref[...], g_ref[...], b_ref[...])


def mlp_ln(x, w1, w2, g, b):
    T = x.shape[0]
    tm = min(TM_MLP, T)
    tf = TF_MLP
    vec = pl.BlockSpec((1, D_MODEL), lambda i, j: (0, 0))
    return pl.pallas_call(
        _mlp_kernel,
        out_shape=jax.ShapeDtypeStruct((T, D_MODEL), F32),
        grid=(T // tm, D_FF // tf),
        in_specs=[pl.BlockSpec((tm, D_MODEL), lambda i, j: (i, 0)),
                  pl.BlockSpec((D_MODEL, tf), lambda i, j: (0, j)),
                  pl.BlockSpec((tf, D_MODEL), lambda i, j: (j, 0)), vec, vec],
        out_specs=pl.BlockSpec((tm, D_MODEL), lambda i, j: (i, 0)),
        scratch_shapes=[pltpu.VMEM((tm, D_MODEL), F32)],
        compiler_params=_cparams(("parallel", "arbitrary")),
    )(x, w1, w2, g, b)


def _permute_w_in(w_in):
    sizes = (CONV_WIDTH, CONV_WIDTH, CONV_WIDTH, ATT_WIDTH, HEAD_DIM, HEAD_DIM,
             IDX_HEADS * IDX_DIM, IDX_DIM, IDX_HEADS, ATT_WIDTH, ATT_WIDTH, ATT_WIDTH)
    offs = np.concatenate([[0], np.cumsum(sizes)])
    seg = [w_in[:, offs[k]:offs[k + 1]] for k in range(len(sizes))]
    cb, cc, ch, dq, dk, dv, iq, ik, iw, cq, ck, cv = seg
    pad = jnp.zeros((w_in.shape[0], LANES - IDX_DIM - IDX_HEADS), w_in.dtype)
    return jnp.concatenate([cq, ck, cv, dq, cb, cc, ch, iq, dk, dv, ik, iw, pad], axis=1).astype(BF16)


@jax.jit
def kernel(x, mem, positions, w_in, conv_w, rel_bias, w_mix_out, ln1_g, ln1_b,
           w_mq, w_mkv, w_mo, ln2_g, ln2_b, w_ff1, w_ff2, ln3_g, ln3_b):
    B, S, D = x.shape
    T = B * S
    M = mem.shape[1]
    tabs_q = rope_tables(positions, HEAD_DIM, ROPE_DIM)
    tabs_i = rope_tables(positions, IDX_DIM, IDX_ROPE_DIM)
    mem2 = mem.reshape(B * M, D)
    xf = x.reshape(T, D)
    vec = lambda a: a.reshape(1, D)
    for l in range(w_in.shape[0]):
        proj = matmul(xf, _permute_w_in(w_in[l]), min(TM_PROJ, T)).reshape(B, S, PROJ_W)
        yconv, q, kv, iqr, ikw = prep(proj, tabs_q, tabs_i, conv_w[l])
        ydsa = dsa_attention(q, kv, iqr, ikw)
        yband = band_attention(proj, rel_bias[l])
        xf = outproj_ln(yconv.reshape(T, -1), ydsa.reshape(T, -1), yband.reshape(T, -1), xf,
                        w_mix_out[l].astype(BF16), vec(ln1_g[l]), vec(ln1_b[l]))
        kvm = matmul(mem2, w_mkv[l].astype(BF16), min(512, B * M)).reshape(B, M, 2 * D)
        xf = cross_ln(xf.reshape(B, S, D), kvm[..., :D].astype(BF16), kvm[..., D:].astype(BF16),
                      w_mq[l].astype(BF16), w_mo[l].astype(BF16), vec(ln2_g[l]), vec(ln2_b[l])).reshape(T, D)
        xf = mlp_ln(xf, w_ff1[l].astype(BF16), w_ff2[l].astype(BF16), vec(ln3_g[l]), vec(ln3_b[l]))
    return xf.reshape(B, S, D)
```

```python
import functools

import numpy as np
import jax
import jax.numpy as jnp
from jax import lax
from jax.experimental import pallas as pl
from jax.experimental.pallas import tpu as pltpu

F32 = jnp.float32
BF16 = jnp.bfloat16
I32 = jnp.int32

D_MODEL = 1024
CHUNK = 64
HEAD_DIM = 64
CONV_WIDTH = 256
CONV_K = 3
ATT_WIDTH = 384
DSA_HEADS = 6
CHK_HEADS = 6
ROPE_DIM = 16
ROPE_THETA = 500000.0
IDX_HEADS = 8
IDX_DIM = 32
IDX_ROPE_DIM = 8
TOPK_MAX = 256
BAND_CHUNKS = 8
REL_CLIP = 128
MEM_HEADS = 4
MEM_HEAD_DIM = 256
D_FF = 4096
LN_EPS = 1e-5
DEPTH = 4
ALPHA = (2.0 * DEPTH) ** 0.25

LANES = 128
SUBLANES = 8
VMEM_LIMIT = 52 * 1024 * 1024

COL_CQ, COL_CK, COL_CV = 0, 384, 768
COL_DQ = 1152
COL_CONV = 1536
COL_IQ = 2304
COL_DKV = 2560
COL_IKW = 2688
PROJ_W = 2816

TM_PROJ = 512
TM_PREP = 512
TQ_DSA = 128
TK_DSA = 512
TKA_DSA = 1024
TQ_BAND = 256
TM_OUT = 512
TM_CROSS = 512
TM_MLP = 1024
TF_MLP = 512

LOG2_E = 1.4426950408889634
NEG_BIG = -1e30
INT_MIN = -2147483648


def _cparams(sem):
    return pltpu.CompilerParams(dimension_semantics=sem, vmem_limit_bytes=VMEM_LIMIT)


def _layer_norm(y, g, b):
    mu = jnp.mean(y, axis=-1, keepdims=True)
    d = y - mu
    var = jnp.mean(d * d, axis=-1, keepdims=True)
    return d * lax.rsqrt(var + LN_EPS) * g + b


def _rope_table_kernel(pos_ref, freq_ref, mc_ref, m1_ref, m2_ref, c_ref, s1_ref, s2_ref):
    ang = pos_ref[0].astype(F32) * freq_ref[...]
    c = jnp.cos(ang)
    s = jnp.sin(ang)
    mc = mc_ref[...]
    c_ref[0] = c * mc + (1.0 - mc)
    s1_ref[0] = s * m1_ref[...]
    s2_ref[0] = -(s * m2_ref[...])


def _rope_lane_consts(head_dim, rot_dim):
    half = rot_dim // 2
    inv_freq = np.power(np.float32(ROPE_THETA), -np.arange(half, dtype=np.float32) / np.float32(half)).astype(np.float32)
    lane = np.arange(LANES)
    dd = lane % head_dim
    freq = np.where(dd < rot_dim, inv_freq[lane % half], 0.0).astype(np.float32)
    mc = (dd < rot_dim).astype(np.float32)
    m1 = ((dd >= half) & (dd < rot_dim)).astype(np.float32)
    m2 = (dd < half).astype(np.float32)
    return [jnp.asarray(a.reshape(1, LANES)) for a in (freq, mc, m1, m2)]


def rope_tables(positions, head_dim, rot_dim):
    B, S = positions.shape
    tm = min(1024, S)
    consts = _rope_lane_consts(head_dim, rot_dim)
    row = pl.BlockSpec((1, LANES), lambda b, i: (0, 0))
    tab = pl.BlockSpec((1, tm, LANES), lambda b, i: (b, i, 0))
    out = jax.ShapeDtypeStruct((B, S, LANES), F32)
    return pl.pallas_call(
        _rope_table_kernel,
        out_shape=(out, out, out),
        grid=(B, S // tm),
        in_specs=[pl.BlockSpec((1, tm, 1), lambda b, i: (b, i, 0)), row, row, row, row],
        out_specs=(tab, tab, tab),
        compiler_params=_cparams(("parallel", "parallel")),
    )(positions.reshape(B, S, 1), *consts)


def _mm_kernel(x_ref, w_ref, o_ref):
    o_ref[...] = jnp.dot(x_ref[...].astype(BF16), w_ref[...], preferred_element_type=F32)


def matmul(x, w, tm):
    M, K = x.shape
    N = w.shape[1]
    return pl.pallas_call(
        _mm_kernel,
        out_shape=jax.ShapeDtypeStruct((M, N), F32),
        grid=(M // tm,),
        in_specs=[pl.BlockSpec((tm, K), lambda i: (i, 0)), pl.BlockSpec((K, N), lambda i: (0, 0))],
        out_specs=pl.BlockSpec((tm, N), lambda i: (i, 0)),
        compiler_params=_cparams(("parallel",)),
    )(x, w)


def _rope_apply(x, c, s1, s2, half):
    return x * c + pltpu.roll(x, half, 1) * s1 + pltpu.roll(x, LANES - half, 1) * s2


def _prep_kernel(dq_ref, conv_ref, halo_ref, iq_ref, dkv_ref, ikw_ref,
                 cq_ref, s1q_ref, s2q_ref, ci_ref, s1i_ref, s2i_ref, cw_ref,
                 yconv_ref, q_ref, kt_ref, v_ref, iq3_ref, ik3_ref, ikwr_ref, u_scr):
    tm = dq_ref.shape[1]
    i = pl.program_id(1)
    cv = conv_ref[0]
    gate_b = cv[:, 0:CONV_WIDTH]
    u = cv[:, CONV_WIDTH:2 * CONV_WIDTH] * cv[:, 2 * CONV_WIDTH:3 * CONV_WIDTH]
    hv = halo_ref[0]
    hu = hv[:, CONV_WIDTH:2 * CONV_WIDTH] * hv[:, 2 * CONV_WIDTH:3 * CONV_WIDTH]
    hu = jnp.where(i > 0, hu, 0.0)
    u_scr[0:SUBLANES, :] = hu
    u_scr[SUBLANES:SUBLANES + tm, :] = u
    u1 = u_scr[SUBLANES - 1:SUBLANES - 1 + tm, :]
    u2 = u_scr[SUBLANES - 2:SUBLANES - 2 + tm, :]
    cw = cw_ref[...]
    y = cw[0:1, :] * u2 + cw[1:2, :] * u1 + cw[2:3, :] * u
    yconv_ref[0] = (gate_b * y).astype(BF16)
    cq, s1q, s2q = cq_ref[0], s1q_ref[0], s2q_ref[0]
    ci, s1i, s2i = ci_ref[0], s1i_ref[0], s2i_ref[0]
    hq = ROPE_DIM // 2
    hi = IDX_ROPE_DIM // 2
    dq = dq_ref[0]
    scale = HEAD_DIM ** -0.5 * LOG2_E
    tq = TQ_DSA
    qr = [_rope_apply(dq[:, j * LANES:(j + 1) * LANES], cq, s1q, s2q, hq) * scale for j in range(ATT_WIDTH // LANES)]
    per_tile = LANES // HEAD_DIM
    for blk in range(tm // tq):
        rows = slice(blk * tq, (blk + 1) * tq)
        for h in range(DSA_HEADS):
            c0 = (h % per_tile) * HEAD_DIM
            q_ref[0, blk, h * tq:(h + 1) * tq, :] = qr[h // per_tile][rows, c0:c0 + HEAD_DIM].astype(BF16)
    lane = lax.broadcasted_iota(I32, (1, LANES), 1)
    is_k = lane < HEAD_DIM
    kvr = _rope_apply(dkv_ref[0], jnp.where(is_k, cq, 1.0), jnp.where(is_k, s1q, 0.0),
                      jnp.where(is_k, s2q, 0.0), hq)
    kt_ref[0, 0] = kvr.T[0:HEAD_DIM, :].astype(BF16)
    v_ref[0, 0] = kvr[:, HEAD_DIM:2 * HEAD_DIM].astype(BF16)

    def hi_lo(x):
        x_hi = x.astype(BF16).astype(F32)
        return x_hi, x - x_hi

    iq = iq_ref[0]
    iqr = [_rope_apply(iq[:, j * LANES:(j + 1) * LANES], ci, s1i, s2i, hi) for j in range(IDX_HEADS * IDX_DIM // LANES)]
    per_tile = LANES // IDX_DIM
    zero_q = jnp.zeros((tq, IDX_DIM), F32)
    for blk in range(tm // tq):
        rows = slice(blk * tq, (blk + 1) * tq)
        for h in range(IDX_HEADS):
            c0 = (h % per_tile) * IDX_DIM
            x_hi, x_lo = hi_lo(iqr[h // per_tile][rows, c0:c0 + IDX_DIM])
            iq3_ref[0, blk, h * tq:(h + 1) * tq, :] = jnp.concatenate([x_hi, x_lo, x_hi, zero_q], axis=1).astype(BF16)
    is_ik = lane < IDX_DIM
    is_iw = (lane >= IDX_DIM) & (lane < IDX_DIM + IDX_HEADS)
    ikw = _rope_apply(ikw_ref[0], jnp.where(is_ik, ci, 1.0), jnp.where(is_ik, s1i, 0.0),
                      jnp.where(is_ik, s2i, 0.0), hi)
    ikwr_ref[0] = ikw * jnp.where(is_iw, IDX_HEADS ** -0.5, 1.0)
    k_hi, k_lo = hi_lo(ikw[:, 0:IDX_DIM])
    ik3 = jnp.concatenate([k_hi, k_hi, k_lo, jnp.zeros((tm, IDX_DIM), F32)], axis=1)
    ik3_ref[0, 0] = ik3.T.astype(BF16)


def prep(proj, tabs_q, tabs_i, conv_w):
    B, S, _ = proj.shape
    tm = min(TM_PREP, S)
    hb = tm // SUBLANES

    def col(width, start):
        return pl.BlockSpec((1, tm, width), lambda b, i: (b, i, start // width))

    tab = pl.BlockSpec((1, tm, LANES), lambda b, i: (b, i, 0))
    halo = pl.BlockSpec((1, SUBLANES, 3 * CONV_WIDTH),
                        lambda b, i: (b, jnp.maximum(i * hb - 1, 0), COL_CONV // (3 * CONV_WIDTH)))

    def out(width, dtype):
        return jax.ShapeDtypeStruct((B, S, width), dtype), pl.BlockSpec((1, tm, width), lambda b, i: (b, i, 0))

    tq, tk, tka = TQ_DSA, TK_DSA, TKA_DSA
    assert tm == tk and tka % tm == 0 and S % tka == 0
    nqb, bq, r = S // tq, tm // tq, tka // tm
    outs = [out(CONV_WIDTH, BF16),
            (jax.ShapeDtypeStruct((B, nqb, DSA_HEADS * tq, HEAD_DIM), BF16),
             pl.BlockSpec((1, bq, DSA_HEADS * tq, HEAD_DIM), lambda b, i: (b, i, 0, 0))),
            (jax.ShapeDtypeStruct((B, S // tka, HEAD_DIM, tka), BF16),
             pl.BlockSpec((1, 1, HEAD_DIM, tm), lambda b, i: (b, i // r, 0, i % r))),
            (jax.ShapeDtypeStruct((B, S // tka, tka, HEAD_DIM), BF16),
             pl.BlockSpec((1, 1, tm, HEAD_DIM), lambda b, i: (b, i // r, i % r, 0))),
            (jax.ShapeDtypeStruct((B, nqb, IDX_HEADS * tq, LANES), BF16),
             pl.BlockSpec((1, bq, IDX_HEADS * tq, LANES), lambda b, i: (b, i, 0, 0))),
            (jax.ShapeDtypeStruct((B, S // tk, LANES, tk), BF16),
             pl.BlockSpec((1, 1, LANES, tk), lambda b, i: (b, i, 0, 0))),
            out(LANES, F32)]
    return pl.pallas_call(
        _prep_kernel,
        out_shape=tuple(o[0] for o in outs),
        grid=(B, S // tm),
        in_specs=[col(ATT_WIDTH, COL_DQ), col(3 * CONV_WIDTH, COL_CONV), halo,
                  col(IDX_HEADS * IDX_DIM, COL_IQ), col(LANES, COL_DKV), col(LANES, COL_IKW),
                  tab, tab, tab, tab, tab, tab,
                  pl.BlockSpec((CONV_K, CONV_WIDTH), lambda b, i: (0, 0))],
        out_specs=tuple(o[1] for o in outs),
        scratch_shapes=[pltpu.VMEM((tm + SUBLANES, CONV_WIDTH), F32)],
        compiler_params=_cparams(("parallel", "parallel")),
    )(proj, proj, proj, proj, proj, proj, *tabs_q, *tabs_i, conv_w)


def _sortable_key_np(v):
    b = np.array([v], dtype=np.float32).view(np.int32)[0]
    return int(b ^ ((b >> 31) & 0x7FFFFFFF))


KEY_LOWEST_FINITE = _sortable_key_np(-3.0e38)
KEY_NEG_INF = _sortable_key_np(-np.inf)
KEY_POS_INF = _sortable_key_np(np.inf)


def _key_of_float(x):
    bits = pltpu.bitcast(x, I32)
    return bits ^ ((bits >> 31) & 0x7FFFFFFF)


def _float_of_key(k):
    return pltpu.bitcast(k ^ ((k >> 31) & 0x7FFFFFFF), F32)


def _double_buffered_tiles(n, step):
    npairs = (n - 1) // 2

    def pair(u, carry):
        step(2 * u, 0, False)
        step(2 * u + 1, 1, False)
        return carry

    lax.fori_loop(0, npairs, pair, 0)
    r = 2 * npairs

    @pl.when(r == n - 2)
    def _():
        step(r, 0, False)
        step(r + 1, 1, True)

    @pl.when(r == n - 1)
    def _():
        step(r, 0, True)


def _dsa_kernel(iq3_ref, iw_ref, ik3_ref, q_ref, kt_ref, v_ref, o_ref,
                sc_scr, sb_scr, lg_scr, iwb_scr, cmax_scr, bias_scr, p_scr,
                m_scr, l_scr, a_scr, acc_scr, *, topk):
    tq = TQ_DSA
    tk = TK_DSA
    tka = TKA_DSA
    nsub = tk // LANES
    nsuba = tka // LANES
    qb = pl.program_id(1)
    nk = (qb + 1) * tq
    ntiles = (nk + tk - 1) // tk
    iw = iw_ref[0]
    for h in range(IDX_HEADS):
        iwb_scr[h] = jnp.broadcast_to(iw[:, IDX_DIM + h:IDX_DIM + h + 1], (tq, LANES))
    cmax_scr[...] = jnp.full(cmax_scr.shape, -jnp.inf, F32)

    def logits(t):
        return jnp.dot(iq3_ref[0, 0], ik3_ref[0, t], preferred_element_type=F32)

    lg_scr[0] = logits(0)

    def score_tile(t, slot, last):
        if last:
            row = lax.broadcasted_iota(I32, (tq, 1), 0)
            lim = ((qb * tq + row) // CHUNK + 1) * CHUNK
            lane = lax.broadcasted_iota(I32, (tq, LANES), 1)
        for j in range(nsub):
            js = slice(j * LANES, (j + 1) * LANES)
            sc = iwb_scr[0] * jnp.maximum(lg_scr[slot, 0:tq, js], 0.0)
            for h in range(1, IDX_HEADS):
                sc = sc + iwb_scr[h] * jnp.maximum(lg_scr[slot, h * tq:(h + 1) * tq, js], 0.0)
            if last:
                sc = jnp.where(t * tk + j * LANES + lane < lim, sc, -jnp.inf)
            cmax_scr[j % 2] = jnp.maximum(cmax_scr[j % 2], sc)
            sc_scr[t, :, js] = sc
            sb_scr[t, :, js] = sc.astype(BF16)
        if not last:
            lg_scr[1 - slot] = logits(t + 1)

    _double_buffered_tiles(ntiles, score_tile)

    @pl.when(ntiles % 2 == 1)
    def _():
        sc_scr[ntiles] = jnp.full((tq, tk), -jnp.inf, F32)

    def bisect(lo, hi, count_ge):
        def cond(c):
            return jnp.max((c[1] > c[0]).astype(I32)) > 0

        def body(c):
            lo, hi = c
            mid = (lo | hi) - ((lo ^ hi) >> 1)
            total = count_ge(mid)
            ge = total >= topk
            return (jnp.where(ge, mid, lo),
                    jnp.where(total == topk, mid, jnp.where(ge, hi, mid - 1)))

        return lax.while_loop(cond, body, (lo, hi))[0]

    cm = cmax_scr[...]
    lo_f = jnp.min(jnp.minimum(cm[0], cm[1]), axis=1, keepdims=True)
    hi_f = jnp.max(jnp.maximum(cm[0], cm[1]), axis=1, keepdims=True)

    def bf16_of_k16(k):
        return _float_of_key((k << 16) | ((k >> 31) & 0xFFFF))

    def count_ge_bf16(mid):
        vb = jnp.broadcast_to(bf16_of_k16(mid), (tq, LANES)).astype(BF16)
        one = jnp.ones((tq, LANES), BF16)
        zero = jnp.zeros((tq, LANES), BF16)

        def count_tile(t, cnt):
            for j in range(nsub):
                cnt = cnt + jnp.where(sb_scr[t, :, j * LANES:(j + 1) * LANES] >= vb, one, zero)
            return cnt

        cnt = lax.fori_loop(0, ntiles, count_tile, zero)
        return jnp.sum(cnt.astype(F32), axis=1, keepdims=True)

    round_bf16 = lambda x: x.astype(BF16).astype(F32)
    k16 = bisect(_key_of_float(round_bf16(lo_f)) >> 16, _key_of_float(round_bf16(hi_f)) >> 16, count_ge_bf16)

    def count_ge_f32(mid):
        vf = jnp.broadcast_to(_float_of_key(mid), (tq, LANES))

        def count_tile(t, cnt):
            for j in range(nsub):
                cnt = cnt + (sc_scr[t, :, j * LANES:(j + 1) * LANES] >= vf).astype(I32)
            return cnt

        cnt = lax.fori_loop(0, ntiles, count_tile, jnp.zeros((tq, LANES), I32))
        return jnp.sum(cnt, axis=1, keepdims=True)

    kv = _key_of_float(bf16_of_k16(k16))
    lo2 = jnp.maximum(jnp.maximum(kv - 0x8001, KEY_NEG_INF), _key_of_float(lo_f))
    hi2 = jnp.minimum(jnp.minimum(kv + 0x8000, KEY_POS_INF), _key_of_float(hi_f))
    thr_key = bisect(lo2, hi2, count_ge_f32)
    thr = _float_of_key(jnp.maximum(thr_key, KEY_LOWEST_FINITE))
    thr_b = jnp.broadcast_to(thr, (tq, LANES))

    m_scr[...] = jnp.full(m_scr.shape, NEG_BIG, F32)
    l_scr[...] = jnp.zeros(l_scr.shape, F32)
    acc_scr[...] = jnp.zeros(acc_scr.shape, F32)
    ratio = tka // tk

    def attn_tile(t, carry):
        s = jnp.dot(q_ref[0, 0], kt_ref[0, t], preferred_element_type=F32)
        for r in range(ratio):
            for j in range(nsub):
                sel = sc_scr[ratio * t + r, :, j * LANES:(j + 1) * LANES] >= thr_b
                c0 = (r * nsub + j) * LANES
                bias_scr[:, c0:c0 + LANES] = jnp.where(sel, 0.0, NEG_BIG)
        for h in range(DSA_HEADS):
            r0 = h * tq
            sb = [s[r0:r0 + tq, j * LANES:(j + 1) * LANES] + bias_scr[:, j * LANES:(j + 1) * LANES]
                  for j in range(nsuba)]
            m_old = m_scr[h]
            m_new = jnp.maximum(m_old, jnp.max(functools.reduce(jnp.maximum, sb), axis=1, keepdims=True))
            a = jnp.exp2(m_old - m_new)
            ps = [jnp.exp2(x - m_new) for x in sb]
            l_scr[h] = a * l_scr[h] + functools.reduce(jnp.add, ps)
            m_scr[h] = m_new
            a_scr[h] = a
            for j in range(nsuba):
                p_scr[r0:r0 + tq, j * LANES:(j + 1) * LANES] = ps[j].astype(BF16)
        pv = jnp.dot(p_scr[...], v_ref[0, t], preferred_element_type=F32)
        for h in range(DSA_HEADS):
            acc_scr[h] = a_scr[h][:, 0:HEAD_DIM] * acc_scr[h] + pv[h * tq:(h + 1) * tq, :]
        return carry

    lax.fori_loop(0, (ntiles + ratio - 1) // ratio, attn_tile, 0)
    o_ref[0] = jnp.concatenate([acc_scr[h] / jnp.sum(l_scr[h], axis=1, keepdims=True) for h in range(DSA_HEADS)],
                               axis=1).astype(BF16)


def dsa_attention(qst, kt, v, iq3, ik3, ikw):
    B, nqb = qst.shape[:2]
    tq, tk, tka = TQ_DSA, TK_DSA, TKA_DSA
    S = nqb * tq
    nkt, nka = S // tk, S // tka
    topk = min(TOPK_MAX, S // 4)
    per_batch = pl.Buffered(1)
    return pl.pallas_call(
        functools.partial(_dsa_kernel, topk=topk),
        out_shape=jax.ShapeDtypeStruct((B, S, ATT_WIDTH), BF16),
        grid=(B, nqb),
        in_specs=[
            pl.BlockSpec((1, 1, IDX_HEADS * tq, LANES), lambda b, i: (b, i, 0, 0)),
            pl.BlockSpec((1, tq, LANES), lambda b, i: (b, i, 0)),
            pl.BlockSpec((1, nkt, LANES, tk), lambda b, i: (b, 0, 0, 0), pipeline_mode=per_batch),
            pl.BlockSpec((1, 1, DSA_HEADS * tq, HEAD_DIM), lambda b, i: (b, i, 0, 0)),
            pl.BlockSpec((1, nka, HEAD_DIM, tka), lambda b, i: (b, 0, 0, 0), pipeline_mode=per_batch),
            pl.BlockSpec((1, nka, tka, HEAD_DIM), lambda b, i: (b, 0, 0, 0), pipeline_mode=per_batch),
        ],
        out_specs=pl.BlockSpec((1, tq, ATT_WIDTH), lambda b, i: (b, i, 0)),
        scratch_shapes=[
            pltpu.VMEM((nkt, tq, tk), F32),
            pltpu.VMEM((nkt, tq, tk), BF16),
            pltpu.VMEM((2, IDX_HEADS * tq, tk), F32),
            pltpu.VMEM((IDX_HEADS, tq, LANES), F32),
            pltpu.VMEM((2, tq, LANES), F32),
            pltpu.VMEM((tq, tka), F32),
            pltpu.VMEM((DSA_HEADS * tq, tka), BF16),
            pltpu.VMEM((DSA_HEADS, tq, LANES), F32),
            pltpu.VMEM((DSA_HEADS, tq, LANES), F32),
            pltpu.VMEM((DSA_HEADS, tq, LANES), F32),
            pltpu.VMEM((DSA_HEADS, tq, HEAD_DIM), F32),
        ],
        compiler_params=_cparams(("parallel", "arbitrary")),
    )(iq3, ikw, ik3, qst, kt, v)


def _band_kernel(q_ref, k0_ref, k1_ref, k2_ref, v0_ref, v1_ref, v2_ref, bias_ref, o_ref):
    tq = TQ_BAND
    i = pl.program_id(1)
    q = (q_ref[0] * (HEAD_DIM ** -0.5)).astype(BF16)
    ks = [r[0].astype(BF16) for r in (k0_ref, k1_ref, k2_ref)]
    vs = [r[0].astype(BF16) for r in (v0_ref, v1_ref, v2_ref)]
    nt = len(ks)
    for h in range(CHK_HEADS):
        hs = slice(h * HEAD_DIM, (h + 1) * HEAD_DIM)
        parts = []
        for j in range(nt):
            s = lax.dot_general(q[:, hs], ks[j][:, hs], (((1,), (1,)), ((), ())), preferred_element_type=F32)
            s = s + bias_ref[h, :, j * tq:(j + 1) * tq]
            if j < nt - 1:
                s = jnp.where(i + j >= nt - 1, s, NEG_BIG)
            parts.append(s)
        m = functools.reduce(jnp.maximum, [jnp.max(s, axis=-1, keepdims=True) for s in parts])
        ps = [jnp.exp(s - m) for s in parts]
        l = functools.reduce(jnp.add, [jnp.sum(p, axis=-1, keepdims=True) for p in ps])
        o = functools.reduce(jnp.add, [jnp.dot(p.astype(BF16), vs[j][:, hs], preferred_element_type=F32)
                                        for j, p in enumerate(ps)])
        o_ref[0, :, hs] = (o / l).astype(BF16)


def _band_bias(rel_bias):
    tq = TQ_BAND
    nt = BAND_CHUNKS * CHUNK // tq + 1
    win = nt * tq
    band = (BAND_CHUNKS + 1) * CHUNK
    H = rel_bias.shape[0]
    n = band + CHUNK - 1
    g = jnp.concatenate([rel_bias, jnp.broadcast_to(rel_bias[:, -1:], (H, n - rel_bias.shape[1]))], axis=1)[:, ::-1]
    hankel = jnp.tile(g, (1, CHUNK + 1))[:, :CHUNK * (n + 1)].reshape(H, CHUNK, n + 1)[:, :, :band]
    w = hankel[:, ::-1, :]
    rows = [jnp.pad(w, ((0, 0), (0, 0), (a * CHUNK, win - band - a * CHUNK)), constant_values=NEG_BIG)
            for a in range(tq // CHUNK)]
    return jnp.concatenate(rows, axis=1).astype(F32)


def band_attention(proj, rel_bias):
    B, S, _ = proj.shape
    tq = TQ_BAND
    nt = BAND_CHUNKS * CHUNK // tq + 1
    bias = _band_bias(rel_bias)

    def kspec(col, j):
        return pl.BlockSpec((1, tq, ATT_WIDTH), lambda b, i: (b, jnp.maximum(i + j - (nt - 1), 0), col // ATT_WIDTH))

    return pl.pallas_call(
        _band_kernel,
        out_shape=jax.ShapeDtypeStruct((B, S, ATT_WIDTH), BF16),
        grid=(B, S // tq),
        in_specs=[pl.BlockSpec((1, tq, ATT_WIDTH), lambda b, i: (b, i, COL_CQ // ATT_WIDTH))]
                 + [kspec(COL_CK, j) for j in range(nt)] + [kspec(COL_CV, j) for j in range(nt)]
                 + [pl.BlockSpec((CHK_HEADS, tq, nt * tq), lambda b, i: (0, 0, 0))],
        out_specs=pl.BlockSpec((1, tq, ATT_WIDTH), lambda b, i: (b, i, 0)),
        compiler_params=_cparams(("parallel", "parallel")),
    )(proj, proj, proj, proj, proj, proj, proj, bias)


def _outproj_kernel(yc_ref, yd_ref, yb_ref, x_ref, w_ref, g_ref, b_ref, o_ref):
    c0, c1 = CONV_WIDTH, CONV_WIDTH + ATT_WIDTH
    mix = jnp.dot(yc_ref[...], w_ref[0:c0, :], preferred_element_type=F32)
    mix = mix + jnp.dot(yd_ref[...], w_ref[c0:c1, :], preferred_element_type=F32)
    mix = mix + jnp.dot(yb_ref[...], w_ref[c1:D_MODEL, :], preferred_element_type=F32)
    o_ref[...] = _layer_norm(ALPHA * x_ref[...] + mix, g_ref[...], b_ref[...])


def outproj_ln(yc, yd, yb, x, w, g, b):
    T = x.shape[0]
    tm = min(TM_OUT, T)
    rows = lambda width: pl.BlockSpec((tm, width), lambda i: (i, 0))
    vec = pl.BlockSpec((1, D_MODEL), lambda i: (0, 0))
    return pl.pallas_call(
        _outproj_kernel,
        out_shape=jax.ShapeDtypeStruct((T, D_MODEL), F32),
        grid=(T // tm,),
        in_specs=[rows(CONV_WIDTH), rows(ATT_WIDTH), rows(ATT_WIDTH), rows(D_MODEL),
                  pl.BlockSpec((D_MODEL, D_MODEL), lambda i: (0, 0)), vec, vec],
        out_specs=rows(D_MODEL),
        compiler_params=_cparams(("parallel",)),
    )(yc, yd, yb, x, w, g, b)


def _cross_kernel(x_ref, wq_ref, k_ref, v_ref, wo_ref, g_ref, b_ref, o_ref):
    x = x_ref[0]
    q = jnp.dot(x.astype(BF16), wq_ref[...], preferred_element_type=F32)
    q = (q * (MEM_HEAD_DIM ** -0.5)).astype(BF16)
    k = k_ref[0]
    v = v_ref[0]
    outs = []
    for h in range(MEM_HEADS):
        hs = slice(h * MEM_HEAD_DIM, (h + 1) * MEM_HEAD_DIM)
        s = lax.dot_general(q[:, hs], k[:, hs], (((1,), (1,)), ((), ())), preferred_element_type=F32)
        p = jnp.exp(s - jnp.max(s, axis=-1, keepdims=True))
        p = p / jnp.sum(p, axis=-1, keepdims=True)
        outs.append(jnp.dot(p.astype(BF16), v[:, hs], preferred_element_type=F32).astype(BF16))
    o = jnp.concatenate(outs, axis=-1)
    cross = jnp.dot(o, wo_ref[...], preferred_element_type=F32)
    o_ref[0] = _layer_norm(ALPHA * x + cross, g_ref[...], b_ref[...])


def cross_ln(x, k, v, wq, wo, g, b):
    B, S, _ = x.shape
    M = k.shape[1]
    tm = min(TM_CROSS, S)
    full = lambda shape: pl.BlockSpec(shape, lambda bb, i: (0,) * len(shape))
    return pl.pallas_call(
        _cross_kernel,
        out_shape=jax.ShapeDtypeStruct((B, S, D_MODEL), F32),
        grid=(B, S // tm),
        in_specs=[pl.BlockSpec((1, tm, D_MODEL), lambda bb, i: (bb, i, 0)),
                  full((D_MODEL, D_MODEL)),
                  pl.BlockSpec((1, M, D_MODEL), lambda bb, i: (bb, 0, 0)),
                  pl.BlockSpec((1, M, D_MODEL), lambda bb, i: (bb, 0, 0)),
                  full((D_MODEL, D_MODEL)), full((1, D_MODEL)), full((1, D_MODEL))],
        out_specs=pl.BlockSpec((1, tm, D_MODEL), lambda bb, i: (bb, i, 0)),
        compiler_params=_cparams(("parallel", "parallel")),
    )(x, wq, k, v, wo, g, b)


def _mlp_kernel(x_ref, w1_ref, w2_ref, g_ref, b_ref, o_ref, acc_ref):
    j = pl.program_id(1)

    @pl.when(j == 0)
    def _():
        acc_ref[...] = jnp.zeros(acc_ref.shape, F32)

    h = jnp.dot(x_ref[...].astype(BF16), w1_ref[...], preferred_element_type=F32)
    h = jnp.square(jnp.maximum(h, 0.0)).astype(BF16)
    acc_ref[...] += jnp.dot(h, w2_ref[...], preferred_element_type=F32)

    @pl.when(j == pl.num_programs(1) - 1)
    def _():
        o_ref[...] = _layer_norm(ALPHA * x_ref[...] + acc_ref[...], g_ref[...], b_ref[...])


def mlp_ln(x, w1, w2, g, b):
    T = x.shape[0]
    tm = min(TM_MLP, T)
    tf = TF_MLP
    vec = pl.BlockSpec((1, D_MODEL), lambda i, j: (0, 0))
    return pl.pallas_call(
        _mlp_kernel,
        out_shape=jax.ShapeDtypeStruct((T, D_MODEL), F32),
        grid=(T // tm, D_FF // tf),
        in_specs=[pl.BlockSpec((tm, D_MODEL), lambda i, j: (i, 0)),
                  pl.BlockSpec((D_MODEL, tf), lambda i, j: (0, j)),
                  pl.BlockSpec((tf, D_MODEL), lambda i, j: (j, 0)), vec, vec],
        out_specs=pl.BlockSpec((tm, D_MODEL), lambda i, j: (i, 0)),
        scratch_shapes=[pltpu.VMEM((tm, D_MODEL), F32)],
        compiler_params=_cparams(("parallel", "arbitrary")),
    )(x, w1, w2, g, b)


def _permute_w_in(w_in):
    sizes = (CONV_WIDTH, CONV_WIDTH, CONV_WIDTH, ATT_WIDTH, HEAD_DIM, HEAD_DIM,
             IDX_HEADS * IDX_DIM, IDX_DIM, IDX_HEADS, ATT_WIDTH, ATT_WIDTH, ATT_WIDTH)
    offs = np.concatenate([[0], np.cumsum(sizes)])
    seg = [w_in[:, offs[k]:offs[k + 1]] for k in range(len(sizes))]
    cb, cc, ch, dq, dk, dv, iq, ik, iw, cq, ck, cv = seg
    pad = jnp.zeros((w_in.shape[0], LANES - IDX_DIM - IDX_HEADS), w_in.dtype)
    return jnp.concatenate([cq, ck, cv, dq, cb, cc, ch, iq, dk, dv, ik, iw, pad], axis=1).astype(BF16)


@jax.jit
def kernel(x, mem, positions, w_in, conv_w, rel_bias, w_mix_out, ln1_g, ln1_b,
           w_mq, w_mkv, w_mo, ln2_g, ln2_b, w_ff1, w_ff2, ln3_g, ln3_b):
    B, S, D = x.shape
    T = B * S
    M = mem.shape[1]
    tabs_q = rope_tables(positions, HEAD_DIM, ROPE_DIM)
    tabs_i = rope_tables(positions, IDX_DIM, IDX_ROPE_DIM)
    mem2 = mem.reshape(B * M, D)
    xf = x.reshape(T, D)
    vec = lambda a: a.reshape(1, D)
    for l in range(w_in.shape[0]):
        proj = matmul(xf, _permute_w_in(w_in[l]), min(TM_PROJ, T)).reshape(B, S, PROJ_W)
        yconv, qst, kt, v, iq3, ik3, ikw = prep(proj, tabs_q, tabs_i, conv_w[l])
        ydsa = dsa_attention(qst, kt, v, iq3, ik3, ikw)
        yband = band_attention(proj, rel_bias[l])
        xf = outproj_ln(yconv.reshape(T, -1), ydsa.reshape(T, -1), yband.reshape(T, -1), xf,
                        w_mix_out[l].astype(BF16), vec(ln1_g[l]), vec(ln1_b[l]))
        kvm = matmul(mem2, w_mkv[l].astype(BF16), min(512, B * M)).reshape(B, M, 2 * D)
        xf = cross_ln(xf.reshape(B, S, D), kvm[..., :D].astype(BF16), kvm[..., D:].astype(BF16),
                      w_mq[l].astype(BF16), w_mo[l].astype(BF16), vec(ln2_g[l]), vec(ln2_b[l])).reshape(T, D)
        xf = mlp_ln(xf, w_ff1[l].astype(BF16), w_ff2[l].astype(BF16), vec(ln3_g[l]), vec(ln3_b[l]))
    return xf.reshape(B, S, D)
```

```python
import functools

import numpy as np
import jax
import jax.numpy as jnp
from jax import lax
from jax.experimental import pallas as pl
from jax.experimental.pallas import tpu as pltpu

F32 = jnp.float32
BF16 = jnp.bfloat16
I32 = jnp.int32

D_MODEL = 1024
CHUNK = 64
HEAD_DIM = 64
CONV_WIDTH = 256
CONV_K = 3
ATT_WIDTH = 384
DSA_HEADS = 6
CHK_HEADS = 6
ROPE_DIM = 16
ROPE_THETA = 500000.0
IDX_HEADS = 8
IDX_DIM = 32
IDX_ROPE_DIM = 8
TOPK_MAX = 256
BAND_CHUNKS = 8
REL_CLIP = 128
MEM_HEADS = 4
MEM_HEAD_DIM = 256
D_FF = 4096
LN_EPS = 1e-5
DEPTH = 4
ALPHA = (2.0 * DEPTH) ** 0.25

LANES = 128
SUBLANES = 8
VMEM_LIMIT = 52 * 1024 * 1024

COL_CQ, COL_CK, COL_CV = 0, 384, 768
COL_DQ = 1152
COL_CONV = 1536
COL_IQ = 2304
COL_DKV = 2560
COL_IKW = 2688
PROJ_W = 2816

TM_PROJ = 512
TM_PREP = 512
TQ_DSA = 128
TK_DSA = 512
TKA_DSA = 1024
TQ_BAND = 256
TM_OUT = 512
TM_CROSS = 512
TM_MLP = 1024
TF_MLP = 512

LOG2_E = 1.4426950408889634
NEG_BIG = -1e30
ALL_TIES = 1e9
BISECT_VALUE_STEPS = 12
SNAP_FIRST_STEP = 14
SNAP_EVERY = 6


def _cparams(sem):
    return pltpu.CompilerParams(dimension_semantics=sem, vmem_limit_bytes=VMEM_LIMIT)


def _layer_norm(y, g, b):
    mu = jnp.mean(y, axis=-1, keepdims=True)
    d = y - mu
    var = jnp.mean(d * d, axis=-1, keepdims=True)
    return d * lax.rsqrt(var + LN_EPS) * g + b


def _rope_table_kernel(pos_ref, freq_ref, mc_ref, m1_ref, m2_ref, c_ref, s1_ref, s2_ref):
    ang = pos_ref[0].astype(F32) * freq_ref[...]
    c = jnp.cos(ang)
    s = jnp.sin(ang)
    mc = mc_ref[...]
    c_ref[0] = c * mc + (1.0 - mc)
    s1_ref[0] = s * m1_ref[...]
    s2_ref[0] = -(s * m2_ref[...])


def _rope_lane_consts(head_dim, rot_dim):
    half = rot_dim // 2
    inv_freq = np.power(np.float32(ROPE_THETA), -np.arange(half, dtype=np.float32) / np.float32(half)).astype(np.float32)
    lane = np.arange(LANES)
    dd = lane % head_dim
    freq = np.where(dd < rot_dim, inv_freq[lane % half], 0.0).astype(np.float32)
    mc = (dd < rot_dim).astype(np.float32)
    m1 = ((dd >= half) & (dd < rot_dim)).astype(np.float32)
    m2 = (dd < half).astype(np.float32)
    return [jnp.asarray(a.reshape(1, LANES)) for a in (freq, mc, m1, m2)]


def rope_tables(positions, head_dim, rot_dim):
    B, S = positions.shape
    tm = min(1024, S)
    consts = _rope_lane_consts(head_dim, rot_dim)
    row = pl.BlockSpec((1, LANES), lambda b, i: (0, 0))
    tab = pl.BlockSpec((1, tm, LANES), lambda b, i: (b, i, 0))
    out = jax.ShapeDtypeStruct((B, S, LANES), F32)
    return pl.pallas_call(
        _rope_table_kernel,
        out_shape=(out, out, out),
        grid=(B, S // tm),
        in_specs=[pl.BlockSpec((1, tm, 1), lambda b, i: (b, i, 0)), row, row, row, row],
        out_specs=(tab, tab, tab),
        compiler_params=_cparams(("parallel", "parallel")),
    )(positions.reshape(B, S, 1), *consts)


def _mm_kernel(x_ref, w_ref, o_ref):
    o_ref[...] = jnp.dot(x_ref[...].astype(BF16), w_ref[...], preferred_element_type=F32)


def matmul(x, w, tm):
    M, K = x.shape
    N = w.shape[1]
    return pl.pallas_call(
        _mm_kernel,
        out_shape=jax.ShapeDtypeStruct((M, N), F32),
        grid=(M // tm,),
        in_specs=[pl.BlockSpec((tm, K), lambda i: (i, 0)), pl.BlockSpec((K, N), lambda i: (0, 0))],
        out_specs=pl.BlockSpec((tm, N), lambda i: (i, 0)),
        compiler_params=_cparams(("parallel",)),
    )(x, w)


def _rope_apply(x, c, s1, s2, half):
    return x * c + pltpu.roll(x, half, 1) * s1 + pltpu.roll(x, LANES - half, 1) * s2


def _prep_kernel(dq_ref, conv_ref, halo_ref, iq_ref, dkv_ref, ikw_ref,
                 cq_ref, s1q_ref, s2q_ref, ci_ref, s1i_ref, s2i_ref, cw_ref,
                 yconv_ref, q_ref, kt_ref, v_ref, iq3_ref, ik3_ref, ikwr_ref, u_scr):
    tm = dq_ref.shape[1]
    i = pl.program_id(1)
    cv = conv_ref[0]
    gate_b = cv[:, 0:CONV_WIDTH]
    u = cv[:, CONV_WIDTH:2 * CONV_WIDTH] * cv[:, 2 * CONV_WIDTH:3 * CONV_WIDTH]
    hv = halo_ref[0]
    hu = hv[:, CONV_WIDTH:2 * CONV_WIDTH] * hv[:, 2 * CONV_WIDTH:3 * CONV_WIDTH]
    hu = jnp.where(i > 0, hu, 0.0)
    u_scr[0:SUBLANES, :] = hu
    u_scr[SUBLANES:SUBLANES + tm, :] = u
    u1 = u_scr[SUBLANES - 1:SUBLANES - 1 + tm, :]
    u2 = u_scr[SUBLANES - 2:SUBLANES - 2 + tm, :]
    cw = cw_ref[...]
    y = cw[0:1, :] * u2 + cw[1:2, :] * u1 + cw[2:3, :] * u
    yconv_ref[0] = (gate_b * y).astype(BF16)
    cq, s1q, s2q = cq_ref[0], s1q_ref[0], s2q_ref[0]
    ci, s1i, s2i = ci_ref[0], s1i_ref[0], s2i_ref[0]
    hq = ROPE_DIM // 2
    hi = IDX_ROPE_DIM // 2
    dq = dq_ref[0]
    scale = HEAD_DIM ** -0.5 * LOG2_E
    tq = TQ_DSA
    qr = [_rope_apply(dq[:, j * LANES:(j + 1) * LANES], cq, s1q, s2q, hq) * scale for j in range(ATT_WIDTH // LANES)]
    per_tile = LANES // HEAD_DIM
    for blk in range(tm // tq):
        rows = slice(blk * tq, (blk + 1) * tq)
        for h in range(DSA_HEADS):
            c0 = (h % per_tile) * HEAD_DIM
            q_ref[0, blk, h * tq:(h + 1) * tq, :] = qr[h // per_tile][rows, c0:c0 + HEAD_DIM].astype(BF16)
    lane = lax.broadcasted_iota(I32, (1, LANES), 1)
    is_k = lane < HEAD_DIM
    kvr = _rope_apply(dkv_ref[0], jnp.where(is_k, cq, 1.0), jnp.where(is_k, s1q, 0.0),
                      jnp.where(is_k, s2q, 0.0), hq)
    kt_ref[0, 0] = kvr.T[0:HEAD_DIM, :].astype(BF16)
    v_ref[0, 0] = kvr[:, HEAD_DIM:2 * HEAD_DIM].astype(BF16)

    def hi_lo(x):
        x_hi = x.astype(BF16).astype(F32)
        return x_hi, x - x_hi

    iq = iq_ref[0]
    iqr = [_rope_apply(iq[:, j * LANES:(j + 1) * LANES], ci, s1i, s2i, hi) for j in range(IDX_HEADS * IDX_DIM // LANES)]
    per_tile = LANES // IDX_DIM
    zero_q = jnp.zeros((tq, IDX_DIM), F32)
    for blk in range(tm // tq):
        rows = slice(blk * tq, (blk + 1) * tq)
        for h in range(IDX_HEADS):
            c0 = (h % per_tile) * IDX_DIM
            x_hi, x_lo = hi_lo(iqr[h // per_tile][rows, c0:c0 + IDX_DIM])
            iq3_ref[0, blk, h * tq:(h + 1) * tq, :] = jnp.concatenate([x_hi, x_lo, x_hi, zero_q], axis=1).astype(BF16)
    is_ik = lane < IDX_DIM
    is_iw = (lane >= IDX_DIM) & (lane < IDX_DIM + IDX_HEADS)
    ikw = _rope_apply(ikw_ref[0], jnp.where(is_ik, ci, 1.0), jnp.where(is_ik, s1i, 0.0),
                      jnp.where(is_ik, s2i, 0.0), hi)
    ikwr_ref[0] = ikw * jnp.where(is_iw, IDX_HEADS ** -0.5, 1.0)
    k_hi, k_lo = hi_lo(ikw[:, 0:IDX_DIM])
    ik3 = jnp.concatenate([k_hi, k_hi, k_lo, jnp.zeros((tm, IDX_DIM), F32)], axis=1)
    ik3_ref[0, 0] = ik3.T.astype(BF16)


def prep(proj, tabs_q, tabs_i, conv_w):
    B, S, _ = proj.shape
    tm = min(TM_PREP, S)
    hb = tm // SUBLANES

    def col(width, start):
        return pl.BlockSpec((1, tm, width), lambda b, i: (b, i, start // width))

    tab = pl.BlockSpec((1, tm, LANES), lambda b, i: (b, i, 0))
    halo = pl.BlockSpec((1, SUBLANES, 3 * CONV_WIDTH),
                        lambda b, i: (b, jnp.maximum(i * hb - 1, 0), COL_CONV // (3 * CONV_WIDTH)))

    def out(width, dtype):
        return jax.ShapeDtypeStruct((B, S, width), dtype), pl.BlockSpec((1, tm, width), lambda b, i: (b, i, 0))

    tq, tk, tka = TQ_DSA, TK_DSA, TKA_DSA
    assert tm == tk and tka % tm == 0 and S % tka == 0
    nqb, bq, r = S // tq, tm // tq, tka // tm
    outs = [out(CONV_WIDTH, BF16),
            (jax.ShapeDtypeStruct((B, nqb, DSA_HEADS * tq, HEAD_DIM), BF16),
             pl.BlockSpec((1, bq, DSA_HEADS * tq, HEAD_DIM), lambda b, i: (b, i, 0, 0))),
            (jax.ShapeDtypeStruct((B, S // tka, HEAD_DIM, tka), BF16),
             pl.BlockSpec((1, 1, HEAD_DIM, tm), lambda b, i: (b, i // r, 0, i % r))),
            (jax.ShapeDtypeStruct((B, S // tka, tka, HEAD_DIM), BF16),
             pl.BlockSpec((1, 1, tm, HEAD_DIM), lambda b, i: (b, i // r, i % r, 0))),
            (jax.ShapeDtypeStruct((B, nqb, IDX_HEADS * tq, LANES), BF16),
             pl.BlockSpec((1, bq, IDX_HEADS * tq, LANES), lambda b, i: (b, i, 0, 0))),
            (jax.ShapeDtypeStruct((B, S // tk, LANES, tk), BF16),
             pl.BlockSpec((1, 1, LANES, tk), lambda b, i: (b, i, 0, 0))),
            out(LANES, F32)]
    return pl.pallas_call(
        _prep_kernel,
        out_shape=tuple(o[0] for o in outs),
        grid=(B, S // tm),
        in_specs=[col(ATT_WIDTH, COL_DQ), col(3 * CONV_WIDTH, COL_CONV), halo,
                  col(IDX_HEADS * IDX_DIM, COL_IQ), col(LANES, COL_DKV), col(LANES, COL_IKW),
                  tab, tab, tab, tab, tab, tab,
                  pl.BlockSpec((CONV_K, CONV_WIDTH), lambda b, i: (0, 0))],
        out_specs=tuple(o[1] for o in outs),
        scratch_shapes=[pltpu.VMEM((tm + SUBLANES, CONV_WIDTH), F32)],
        compiler_params=_cparams(("parallel", "parallel")),
    )(proj, proj, proj, proj, proj, proj, *tabs_q, *tabs_i, conv_w)


def _sortable_key_np(v):
    b = np.array([v], dtype=np.float32).view(np.int32)[0]
    return int(b ^ ((b >> 31) & 0x7FFFFFFF))


KEY_LOWEST_FINITE = _sortable_key_np(-3.0e38)
KEY_NEG_INF = _sortable_key_np(-np.inf)
KEY_POS_INF = _sortable_key_np(np.inf)


def _key_of_float(x):
    bits = pltpu.bitcast(x, I32)
    return bits ^ ((bits >> 31) & 0x7FFFFFFF)


def _float_of_key(k):
    return pltpu.bitcast(k ^ ((k >> 31) & 0x7FFFFFFF), F32)


def _double_buffered_tiles(n, step):
    npairs = (n - 1) // 2

    def pair(u, carry):
        step(2 * u, 0, False)
        step(2 * u + 1, 1, False)
        return carry

    lax.fori_loop(0, npairs, pair, 0)
    r = 2 * npairs

    @pl.when(r == n - 2)
    def _():
        step(r, 0, False)
        step(r + 1, 1, True)

    @pl.when(r == n - 1)
    def _():
        step(r, 0, True)


def _dsa_kernel(iq3_ref, iw_ref, ik3_ref, q_ref, kt_ref, v_ref, tri_ref, o_ref,
                sc_scr, lg_scr, iwb_scr, cmax_scr, p_scr,
                m_scr, l_scr, a_scr, acc_scr, *, topk):
    tq = TQ_DSA
    tk = TK_DSA
    tka = TKA_DSA
    nsub = tk // LANES
    nsuba = tka // LANES
    qb = pl.program_id(1)
    nk = (qb + 1) * tq
    ntiles = (nk + tk - 1) // tk
    iw = iw_ref[0]
    for h in range(IDX_HEADS):
        iwb_scr[h] = jnp.broadcast_to(iw[:, IDX_DIM + h:IDX_DIM + h + 1], (tq, LANES))
    cmax_scr[...] = jnp.full(cmax_scr.shape, -jnp.inf, F32)

    def logits(t):
        return jnp.dot(iq3_ref[0, 0], ik3_ref[0, t], preferred_element_type=F32)

    lg_scr[0] = logits(0)

    def score_tile(t, slot, last):
        if last:
            row = lax.broadcasted_iota(I32, (tq, 1), 0)
            lim = ((qb * tq + row) // CHUNK + 1) * CHUNK
            lane = lax.broadcasted_iota(I32, (tq, LANES), 1)
        for j in range(nsub):
            js = slice(j * LANES, (j + 1) * LANES)
            sc = iwb_scr[0] * jnp.maximum(lg_scr[slot, 0:tq, js], 0.0)
            for h in range(1, IDX_HEADS):
                sc = sc + iwb_scr[h] * jnp.maximum(lg_scr[slot, h * tq:(h + 1) * tq, js], 0.0)
            if last:
                sc = jnp.where(t * tk + j * LANES + lane < lim, sc, -jnp.inf)
            cmax_scr[j % 2] = jnp.maximum(cmax_scr[j % 2], sc)
            sc_scr[t, :, js] = sc
        if not last:
            lg_scr[1 - slot] = logits(t + 1)

    _double_buffered_tiles(ntiles, score_tile)

    def count_ge(mid):
        vf = jnp.broadcast_to(_float_of_key(mid), (tq, LANES))

        def count_tile(t, cnt):
            for j in range(nsub):
                cnt = cnt + (sc_scr[t, :, j * LANES:(j + 1) * LANES] >= vf).astype(I32)
            return cnt

        cnt = lax.fori_loop(0, ntiles, count_tile, jnp.zeros((tq, LANES), I32))
        return jnp.sum(cnt, axis=1, keepdims=True)

    def bisect_step(c):
        it, lo, hi, above, exact = c
        int_mid = (lo | hi) - ((lo ^ hi) >> 1)
        val_mid = _key_of_float(0.5 * _float_of_key(lo) + 0.5 * _float_of_key(hi))
        val_mid = jnp.minimum(hi, jnp.maximum(val_mid, lo + 1))
        mid = jnp.where(it < BISECT_VALUE_STEPS, val_mid, int_mid)
        total = count_ge(mid)
        ge = total >= topk
        eq = total == topk
        return (it + 1, jnp.where(ge, mid, lo), jnp.where(eq, mid, jnp.where(ge, hi, mid - 1)),
                jnp.where(ge, above, total), exact | eq.astype(I32))

    def snap_step(c):
        it, lo, hi, above, exact = c
        lo_b = jnp.broadcast_to(_float_of_key(lo), (tq, LANES))
        hi_b = jnp.broadcast_to(_float_of_key(hi), (tq, LANES))

        def range_tile(t, mm):
            mn, mx = mm
            for j in range(nsub):
                x = sc_scr[t, :, j * LANES:(j + 1) * LANES]
                inside = (x >= lo_b) & (x <= hi_b)
                mn = jnp.minimum(mn, jnp.where(inside, x, jnp.inf))
                mx = jnp.maximum(mx, jnp.where(inside, x, -jnp.inf))
            return mn, mx

        mn, mx = lax.fori_loop(0, ntiles, range_tile,
                               (jnp.full((tq, LANES), jnp.inf, F32), jnp.full((tq, LANES), -jnp.inf, F32)))
        open_ = hi > lo
        lo_n = jnp.where(open_, _key_of_float(jnp.min(mn, axis=1, keepdims=True)), lo)
        hi_n = jnp.where(open_, _key_of_float(jnp.max(mx, axis=1, keepdims=True)), hi)
        return it + 1, lo_n, hi_n, above, exact

    def search_cond(c):
        return jnp.max((c[2] > c[1]).astype(I32)) > 0

    def search_body(c):
        it = c[0]
        is_snap = (it >= SNAP_FIRST_STEP) & ((it - SNAP_FIRST_STEP) % SNAP_EVERY == 0)
        return lax.cond(is_snap, snap_step, bisect_step, c)

    cm = cmax_scr[...]
    lo_f = jnp.min(jnp.minimum(cm[0], cm[1]), axis=1, keepdims=True)
    hi_f = jnp.max(jnp.maximum(cm[0], cm[1]), axis=1, keepdims=True)
    zeros_i = jnp.zeros((tq, 1), I32)
    _, lo, _, above, exact = lax.while_loop(
        search_cond, search_body, (jnp.int32(0), _key_of_float(lo_f), _key_of_float(hi_f), zeros_i, zeros_i))
    thr = _float_of_key(jnp.maximum(lo, KEY_LOWEST_FINITE))
    thr_b = jnp.broadcast_to(thr, (tq, LANES))
    ties_b = jnp.broadcast_to(jnp.where(exact != 0, ALL_TIES, (topk - above).astype(F32)), (tq, LANES))
    any_ties = jnp.max((exact == 0).astype(I32)) > 0

    m_scr[...] = jnp.full(m_scr.shape, NEG_BIG, F32)
    l_scr[...] = jnp.zeros(l_scr.shape, F32)
    acc_scr[...] = jnp.zeros(acc_scr.shape, F32)
    ratio = tka // tk

    def mask_plain(t, carry):
        for j in range(nsub):
            js = slice(j * LANES, (j + 1) * LANES)
            sc_scr[t, :, js] = jnp.where(sc_scr[t, :, js] >= thr_b, 0.0, NEG_BIG)
        return carry

    def mask_with_ties(t, seen):
        xs = [sc_scr[t, :, j * LANES:(j + 1) * LANES] for j in range(nsub)]
        ties = [jnp.where(x == thr_b, 1.0, 0.0) for x in xs]
        counts = [jnp.dot(tie.astype(BF16), tri_ref[...], preferred_element_type=F32) for tie in ties]
        for j in range(nsub):
            prefix, total = counts[j][:, 0:LANES], counts[j][:, LANES:2 * LANES]
            keep = jnp.where(seen + prefix <= ties_b, ties[j], 0.0)
            sc_scr[t, :, j * LANES:(j + 1) * LANES] = jnp.where(
                xs[j] > thr_b, 0.0, jnp.where(keep > 0.5, 0.0, NEG_BIG))
            seen = seen + total
        return seen

    def run_mask_with_ties():
        lax.fori_loop(0, ntiles, mask_with_ties, jnp.zeros((tq, LANES), F32))
        return 0

    def run_mask_plain():
        lax.fori_loop(0, ntiles, mask_plain, 0)
        return 0

    lax.cond(any_ties, run_mask_with_ties, run_mask_plain)

    @pl.when(ntiles % ratio != 0)
    def _():
        sc_scr[ntiles] = jnp.full((tq, tk), NEG_BIG, F32)

    def attn_tile(t, carry):
        s = jnp.dot(q_ref[0, 0], kt_ref[0, t], preferred_element_type=F32)
        for h in range(DSA_HEADS):
            r0 = h * tq
            sb = [s[r0:r0 + tq, j * LANES:(j + 1) * LANES]
                  + sc_scr[ratio * t + j // nsub, :, (j % nsub) * LANES:(j % nsub + 1) * LANES]
                  for j in range(nsuba)]
            m_old = m_scr[h]
            m_new = jnp.maximum(m_old, jnp.max(functools.reduce(jnp.maximum, sb), axis=1, keepdims=True))
            a = jnp.exp2(m_old - m_new)
            ps = [jnp.exp2(x - m_new) for x in sb]
            l_scr[h] = a * l_scr[h] + functools.reduce(jnp.add, ps)
            m_scr[h] = m_new
            a_scr[h] = a
            for j in range(nsuba):
                p_scr[r0:r0 + tq, j * LANES:(j + 1) * LANES] = ps[j].astype(BF16)
        pv = jnp.dot(p_scr[...], v_ref[0, t], preferred_element_type=F32)
        for h in range(DSA_HEADS):
            acc_scr[h] = a_scr[h][:, 0:HEAD_DIM] * acc_scr[h] + pv[h * tq:(h + 1) * tq, :]
        return carry

    lax.fori_loop(0, (ntiles + ratio - 1) // ratio, attn_tile, 0)
    o_ref[0] = jnp.concatenate([acc_scr[h] / jnp.sum(l_scr[h], axis=1, keepdims=True) for h in range(DSA_HEADS)],
                               axis=1).astype(BF16)


def dsa_attention(qst, kt, v, iq3, ik3, ikw):
    B, nqb = qst.shape[:2]
    tq, tk, tka = TQ_DSA, TK_DSA, TKA_DSA
    S = nqb * tq
    nkt, nka = S // tk, S // tka
    topk = min(TOPK_MAX, S // 4)
    per_batch = pl.Buffered(1)
    ones = np.ones((LANES, LANES), np.float32)
    tri = jnp.asarray(np.concatenate([np.triu(ones), ones], axis=1), BF16)
    return pl.pallas_call(
        functools.partial(_dsa_kernel, topk=topk),
        out_shape=jax.ShapeDtypeStruct((B, S, ATT_WIDTH), BF16),
        grid=(B, nqb),
        in_specs=[
            pl.BlockSpec((1, 1, IDX_HEADS * tq, LANES), lambda b, i: (b, i, 0, 0)),
            pl.BlockSpec((1, tq, LANES), lambda b, i: (b, i, 0)),
            pl.BlockSpec((1, nkt, LANES, tk), lambda b, i: (b, 0, 0, 0), pipeline_mode=per_batch),
            pl.BlockSpec((1, 1, DSA_HEADS * tq, HEAD_DIM), lambda b, i: (b, i, 0, 0)),
            pl.BlockSpec((1, nka, HEAD_DIM, tka), lambda b, i: (b, 0, 0, 0), pipeline_mode=per_batch),
            pl.BlockSpec((1, nka, tka, HEAD_DIM), lambda b, i: (b, 0, 0, 0), pipeline_mode=per_batch),
            pl.BlockSpec((LANES, 2 * LANES), lambda b, i: (0, 0)),
        ],
        out_specs=pl.BlockSpec((1, tq, ATT_WIDTH), lambda b, i: (b, i, 0)),
        scratch_shapes=[
            pltpu.VMEM((nkt, tq, tk), F32),
            pltpu.VMEM((2, IDX_HEADS * tq, tk), F32),
            pltpu.VMEM((IDX_HEADS, tq, LANES), F32),
            pltpu.VMEM((2, tq, LANES), F32),
            pltpu.VMEM((DSA_HEADS * tq, tka), BF16),
            pltpu.VMEM((DSA_HEADS, tq, LANES), F32),
            pltpu.VMEM((DSA_HEADS, tq, LANES), F32),
            pltpu.VMEM((DSA_HEADS, tq, LANES), F32),
            pltpu.VMEM((DSA_HEADS, tq, HEAD_DIM), F32),
        ],
        compiler_params=_cparams(("parallel", "arbitrary")),
    )(iq3, ikw, ik3, qst, kt, v, tri)


def _band_kernel(q_ref, k0_ref, k1_ref, k2_ref, v0_ref, v1_ref, v2_ref, bias_ref, o_ref):
    tq = TQ_BAND
    i = pl.program_id(1)
    q = (q_ref[0] * (HEAD_DIM ** -0.5)).astype(BF16)
    ks = [r[0].astype(BF16) for r in (k0_ref, k1_ref, k2_ref)]
    vs = [r[0].astype(BF16) for r in (v0_ref, v1_ref, v2_ref)]
    nt = len(ks)
    for h in range(CHK_HEADS):
        hs = slice(h * HEAD_DIM, (h + 1) * HEAD_DIM)
        parts = []
        for j in range(nt):
            s = lax.dot_general(q[:, hs], ks[j][:, hs], (((1,), (1,)), ((), ())), preferred_element_type=F32)
            s = s + bias_ref[h, :, j * tq:(j + 1) * tq]
            if j < nt - 1:
                s = jnp.where(i + j >= nt - 1, s, NEG_BIG)
            parts.append(s)
        m = functools.reduce(jnp.maximum, [jnp.max(s, axis=-1, keepdims=True) for s in parts])
        ps = [jnp.exp(s - m) for s in parts]
        l = functools.reduce(jnp.add, [jnp.sum(p, axis=-1, keepdims=True) for p in ps])
        o = functools.reduce(jnp.add, [jnp.dot(p.astype(BF16), vs[j][:, hs], preferred_element_type=F32)
                                        for j, p in enumerate(ps)])
        o_ref[0, :, hs] = (o / l).astype(BF16)


def _band_bias(rel_bias):
    tq = TQ_BAND
    nt = BAND_CHUNKS * CHUNK // tq + 1
    win = nt * tq
    band = (BAND_CHUNKS + 1) * CHUNK
    H = rel_bias.shape[0]
    n = band + CHUNK - 1
    g = jnp.concatenate([rel_bias, jnp.broadcast_to(rel_bias[:, -1:], (H, n - rel_bias.shape[1]))], axis=1)[:, ::-1]
    hankel = jnp.tile(g, (1, CHUNK + 1))[:, :CHUNK * (n + 1)].reshape(H, CHUNK, n + 1)[:, :, :band]
    w = hankel[:, ::-1, :]
    rows = [jnp.pad(w, ((0, 0), (0, 0), (a * CHUNK, win - band - a * CHUNK)), constant_values=NEG_BIG)
            for a in range(tq // CHUNK)]
    return jnp.concatenate(rows, axis=1).astype(F32)


def band_attention(proj, rel_bias):
    B, S, _ = proj.shape
    tq = TQ_BAND
    nt = BAND_CHUNKS * CHUNK // tq + 1
    bias = _band_bias(rel_bias)

    def kspec(col, j):
        return pl.BlockSpec((1, tq, ATT_WIDTH), lambda b, i: (b, jnp.maximum(i + j - (nt - 1), 0), col // ATT_WIDTH))

    return pl.pallas_call(
        _band_kernel,
        out_shape=jax.ShapeDtypeStruct((B, S, ATT_WIDTH), BF16),
        grid=(B, S // tq),
        in_specs=[pl.BlockSpec((1, tq, ATT_WIDTH), lambda b, i: (b, i, COL_CQ // ATT_WIDTH))]
                 + [kspec(COL_CK, j) for j in range(nt)] + [kspec(COL_CV, j) for j in range(nt)]
                 + [pl.BlockSpec((CHK_HEADS, tq, nt * tq), lambda b, i: (0, 0, 0))],
        out_specs=pl.BlockSpec((1, tq, ATT_WIDTH), lambda b, i: (b, i, 0)),
        compiler_params=_cparams(("parallel", "parallel")),
    )(proj, proj, proj, proj, proj, proj, proj, bias)


def _outproj_kernel(yc_ref, yd_ref, yb_ref, x_ref, w_ref, g_ref, b_ref, o_ref):
    c0, c1 = CONV_WIDTH, CONV_WIDTH + ATT_WIDTH
    mix = jnp.dot(yc_ref[...], w_ref[0:c0, :], preferred_element_type=F32)
    mix = mix + jnp.dot(yd_ref[...], w_ref[c0:c1, :], preferred_element_type=F32)
    mix = mix + jnp.dot(yb_ref[...], w_ref[c1:D_MODEL, :], preferred_element_type=F32)
    o_ref[...] = _layer_norm(ALPHA * x_ref[...] + mix, g_ref[...], b_ref[...])


def outproj_ln(yc, yd, yb, x, w, g, b):
    T = x.shape[0]
    tm = min(TM_OUT, T)
    rows = lambda width: pl.BlockSpec((tm, width), lambda i: (i, 0))
    vec = pl.BlockSpec((1, D_MODEL), lambda i: (0, 0))
    return pl.pallas_call(
        _outproj_kernel,
        out_shape=jax.ShapeDtypeStruct((T, D_MODEL), F32),
        grid=(T // tm,),
        in_specs=[rows(CONV_WIDTH), rows(ATT_WIDTH), rows(ATT_WIDTH), rows(D_MODEL),
                  pl.BlockSpec((D_MODEL, D_MODEL), lambda i: (0, 0)), vec, vec],
        out_specs=rows(D_MODEL),
        compiler_params=_cparams(("parallel",)),
    )(yc, yd, yb, x, w, g, b)


def _cross_kernel(x_ref, wq_ref, k_ref, v_ref, wo_ref, g_ref, b_ref, o_ref):
    x = x_ref[0]
    q = jnp.dot(x.astype(BF16), wq_ref[...], preferred_element_type=F32)
    q = (q * (MEM_HEAD_DIM ** -0.5)).astype(BF16)
    k = k_ref[0]
    v = v_ref[0]
    outs = []
    for h in range(MEM_HEADS):
        hs = slice(h * MEM_HEAD_DIM, (h + 1) * MEM_HEAD_DIM)
        s = lax.dot_general(q[:, hs], k[:, hs], (((1,), (1,)), ((), ())), preferred_element_type=F32)
        p = jnp.exp(s - jnp.max(s, axis=-1, keepdims=True))
        p = p / jnp.sum(p, axis=-1, keepdims=True)
        outs.append(jnp.dot(p.astype(BF16), v[:, hs], preferred_element_type=F32).astype(BF16))
    o = jnp.concatenate(outs, axis=-1)
    cross = jnp.dot(o, wo_ref[...], preferred_element_type=F32)
    o_ref[0] = _layer_norm(ALPHA * x + cross, g_ref[...], b_ref[...])


def cross_ln(x, k, v, wq, wo, g, b):
    B, S, _ = x.shape
    M = k.shape[1]
    tm = min(TM_CROSS, S)
    full = lambda shape: pl.BlockSpec(shape, lambda bb, i: (0,) * len(shape))
    return pl.pallas_call(
        _cross_kernel,
        out_shape=jax.ShapeDtypeStruct((B, S, D_MODEL), F32),
        grid=(B, S // tm),
        in_specs=[pl.BlockSpec((1, tm, D_MODEL), lambda bb, i: (bb, i, 0)),
                  full((D_MODEL, D_MODEL)),
                  pl.BlockSpec((1, M, D_MODEL), lambda bb, i: (bb, 0, 0)),
                  pl.BlockSpec((1, M, D_MODEL), lambda bb, i: (bb, 0, 0)),
                  full((D_MODEL, D_MODEL)), full((1, D_MODEL)), full((1, D_MODEL))],
        out_specs=pl.BlockSpec((1, tm, D_MODEL), lambda bb, i: (bb, i, 0)),
        compiler_params=_cparams(("parallel", "parallel")),
    )(x, wq, k, v, wo, g, b)


def _mlp_kernel(x_ref, w1_ref, w2_ref, g_ref, b_ref, o_ref, acc_ref):
    j = pl.program_id(1)

    @pl.when(j == 0)
    def _():
        acc_ref[...] = jnp.zeros(acc_ref.shape, F32)

    h = jnp.dot(x_ref[...].astype(BF16), w1_ref[...], preferred_element_type=F32)
    h = jnp.square(jnp.maximum(h, 0.0)).astype(BF16)
    acc_ref[...] += jnp.dot(h, w2_ref[...], preferred_element_type=F32)

    @pl.when(j == pl.num_programs(1) - 1)
    def _():
        o_ref[...] = _layer_norm(ALPHA * x_ref[...] + acc_ref[...], g_ref[...], b_ref[...])


def mlp_ln(x, w1, w2, g, b):
    T = x.shape[0]
    tm = min(TM_MLP, T)
    tf = TF_MLP
    vec = pl.BlockSpec((1, D_MODEL), lambda i, j: (0, 0))
    return pl.pallas_call(
        _mlp_kernel,
        out_shape=jax.ShapeDtypeStruct((T, D_MODEL), F32),
        grid=(T // tm, D_FF // tf),
        in_specs=[pl.BlockSpec((tm, D_MODEL), lambda i, j: (i, 0)),
                  pl.BlockSpec((D_MODEL, tf), lambda i, j: (0, j)),
                  pl.BlockSpec((tf, D_MODEL), lambda i, j: (j, 0)), vec, vec],
        out_specs=pl.BlockSpec((tm, D_MODEL), lambda i, j: (i, 0)),
        scratch_shapes=[pltpu.VMEM((tm, D_MODEL), F32)],
        compiler_params=_cparams(("parallel", "arbitrary")),
    )(x, w1, w2, g, b)


def _permute_w_in(w_in):
    sizes = (CONV_WIDTH, CONV_WIDTH, CONV_WIDTH, ATT_WIDTH, HEAD_DIM, HEAD_DIM,
             IDX_HEADS * IDX_DIM, IDX_DIM, IDX_HEADS, ATT_WIDTH, ATT_WIDTH, ATT_WIDTH)
    offs = np.concatenate([[0], np.cumsum(sizes)])
    seg = [w_in[:, offs[k]:offs[k + 1]] for k in range(len(sizes))]
    cb, cc, ch, dq, dk, dv, iq, ik, iw, cq, ck, cv = seg
    pad = jnp.zeros((w_in.shape[0], LANES - IDX_DIM - IDX_HEADS), w_in.dtype)
    return jnp.concatenate([cq, ck, cv, dq, cb, cc, ch, iq, dk, dv, ik, iw, pad], axis=1).astype(BF16)


@jax.jit
def kernel(x, mem, positions, w_in, conv_w, rel_bias, w_mix_out, ln1_g, ln1_b,
           w_mq, w_mkv, w_mo, ln2_g, ln2_b, w_ff1, w_ff2, ln3_g, ln3_b):
    B, S, D = x.shape
    T = B * S
    M = mem.shape[1]
    tabs_q = rope_tables(positions, HEAD_DIM, ROPE_DIM)
    tabs_i = rope_tables(positions, IDX_DIM, IDX_ROPE_DIM)
    mem2 = mem.reshape(B * M, D)
    xf = x.reshape(T, D)
    vec = lambda a: a.reshape(1, D)
    for l in range(w_in.shape[0]):
        proj = matmul(xf, _permute_w_in(w_in[l]), min(TM_PROJ, T)).reshape(B, S, PROJ_W)
        yconv, qst, kt, v, iq3, ik3, ikw = prep(proj, tabs_q, tabs_i, conv_w[l])
        ydsa = dsa_attention(qst, kt, v, iq3, ik3, ikw)
        yband = band_attention(proj, rel_bias[l])
        xf = outproj_ln(yconv.reshape(T, -1), ydsa.reshape(T, -1), yband.reshape(T, -1), xf,
                        w_mix_out[l].astype(BF16), vec(ln1_g[l]), vec(ln1_b[l]))
        kvm = matmul(mem2, w_mkv[l].astype(BF16), min(512, B * M)).reshape(B, M, 2 * D)
        xf = cross_ln(xf.reshape(B, S, D), kvm[..., :D].astype(BF16), kvm[..., D:].astype(BF16),
                      w_mq[l].astype(BF16), w_mo[l].astype(BF16), vec(ln2_g[l]), vec(ln2_b[l])).reshape(T, D)
        xf = mlp_ln(xf, w_ff1[l].astype(BF16), w_ff2[l].astype(BF16), vec(ln3_g[l]), vec(ln3_b[l]))
    return xf.reshape(B, S, D)
```

```python
import functools

import numpy as np
import jax
import jax.numpy as jnp
from jax import lax
from jax.experimental import pallas as pl
from jax.experimental.pallas import tpu as pltpu

F32 = jnp.float32
BF16 = jnp.bfloat16
I32 = jnp.int32

D_MODEL = 1024
CHUNK = 64
HEAD_DIM = 64
CONV_WIDTH = 256
CONV_K = 3
ATT_WIDTH = 384
DSA_HEADS = 6
CHK_HEADS = 6
ROPE_DIM = 16
ROPE_THETA = 500000.0
IDX_HEADS = 8
IDX_DIM = 32
IDX_ROPE_DIM = 8
TOPK_MAX = 256
BAND_CHUNKS = 8
REL_CLIP = 128
MEM_HEADS = 4
MEM_HEAD_DIM = 256
D_FF = 4096
LN_EPS = 1e-5
DEPTH = 4
ALPHA = (2.0 * DEPTH) ** 0.25

LANES = 128
SUBLANES = 8
VMEM_LIMIT = 52 * 1024 * 1024

COL_CQ, COL_CK, COL_CV = 0, 384, 768
COL_DQ = 1152
COL_CONV = 1536
COL_IQ = 2304
COL_DKV = 2560
COL_IKW = 2688
PROJ_W = 2816

TM_PROJ = 512
TM_PREP = 512
TQ_DSA = 128
TK_DSA = 512
TKA_DSA = 1024
TQ_BAND = 256
TM_OUT = 512
TM_CROSS = 512
TM_MLP = 1024
TF_MLP = 1024

LOG2_E = 1.4426950408889634
NEG_BIG = -1e30
ALL_TIES = 1e9
MAX_STALE_JUMP = 100.0
BISECT_VALUE_STEPS = 12
SNAP_FIRST_STEP = 14
SNAP_EVERY = 6


def _cparams(sem):
    return pltpu.CompilerParams(dimension_semantics=sem, vmem_limit_bytes=VMEM_LIMIT)


def _layer_norm(y, g, b):
    mu = jnp.mean(y, axis=-1, keepdims=True)
    d = y - mu
    var = jnp.mean(d * d, axis=-1, keepdims=True)
    return d * lax.rsqrt(var + LN_EPS) * g + b


def _rope_table_kernel(pos_ref, freq_ref, mc_ref, m1_ref, m2_ref, c_ref, s1_ref, s2_ref):
    ang = pos_ref[0].astype(F32) * freq_ref[...]
    c = jnp.cos(ang)
    s = jnp.sin(ang)
    mc = mc_ref[...]
    c_ref[0] = c * mc + (1.0 - mc)
    s1_ref[0] = s * m1_ref[...]
    s2_ref[0] = -(s * m2_ref[...])


def _rope_lane_consts(head_dim, rot_dim):
    half = rot_dim // 2
    inv_freq = np.power(np.float32(ROPE_THETA), -np.arange(half, dtype=np.float32) / np.float32(half)).astype(np.float32)
    lane = np.arange(LANES)
    dd = lane % head_dim
    freq = np.where(dd < rot_dim, inv_freq[lane % half], 0.0).astype(np.float32)
    mc = (dd < rot_dim).astype(np.float32)
    m1 = ((dd >= half) & (dd < rot_dim)).astype(np.float32)
    m2 = (dd < half).astype(np.float32)
    return [jnp.asarray(a.reshape(1, LANES)) for a in (freq, mc, m1, m2)]


def rope_tables(positions, head_dim, rot_dim):
    B, S = positions.shape
    tm = min(1024, S)
    consts = _rope_lane_consts(head_dim, rot_dim)
    row = pl.BlockSpec((1, LANES), lambda b, i: (0, 0))
    tab = pl.BlockSpec((1, tm, LANES), lambda b, i: (b, i, 0))
    out = jax.ShapeDtypeStruct((B, S, LANES), F32)
    return pl.pallas_call(
        _rope_table_kernel,
        out_shape=(out, out, out),
        grid=(B, S // tm),
        in_specs=[pl.BlockSpec((1, tm, 1), lambda b, i: (b, i, 0)), row, row, row, row],
        out_specs=(tab, tab, tab),
        compiler_params=_cparams(("parallel", "parallel")),
    )(positions.reshape(B, S, 1), *consts)


def _mm_kernel(x_ref, w_ref, o_ref):
    o_ref[...] = jnp.dot(x_ref[...].astype(BF16), w_ref[...], preferred_element_type=F32)


def matmul(x, w, tm):
    M, K = x.shape
    N = w.shape[1]
    return pl.pallas_call(
        _mm_kernel,
        out_shape=jax.ShapeDtypeStruct((M, N), F32),
        grid=(M // tm,),
        in_specs=[pl.BlockSpec((tm, K), lambda i: (i, 0)), pl.BlockSpec((K, N), lambda i: (0, 0))],
        out_specs=pl.BlockSpec((tm, N), lambda i: (i, 0)),
        compiler_params=_cparams(("parallel",)),
    )(x, w)


def _rope_apply(x, c, s1, s2, half):
    return x * c + pltpu.roll(x, half, 1) * s1 + pltpu.roll(x, LANES - half, 1) * s2


def _prep_kernel(dq_ref, conv_ref, halo_ref, iq_ref, dkv_ref, ikw_ref,
                 cq_ref, s1q_ref, s2q_ref, ci_ref, s1i_ref, s2i_ref, cw_ref,
                 yconv_ref, q_ref, kt_ref, v_ref, iq3_ref, ik3_ref, ikwr_ref, u_scr):
    tm = dq_ref.shape[1]
    i = pl.program_id(1)
    cv = conv_ref[0]
    gate_b = cv[:, 0:CONV_WIDTH]
    u = cv[:, CONV_WIDTH:2 * CONV_WIDTH] * cv[:, 2 * CONV_WIDTH:3 * CONV_WIDTH]
    hv = halo_ref[0]
    hu = hv[:, CONV_WIDTH:2 * CONV_WIDTH] * hv[:, 2 * CONV_WIDTH:3 * CONV_WIDTH]
    hu = jnp.where(i > 0, hu, 0.0)
    u_scr[0:SUBLANES, :] = hu
    u_scr[SUBLANES:SUBLANES + tm, :] = u
    u1 = u_scr[SUBLANES - 1:SUBLANES - 1 + tm, :]
    u2 = u_scr[SUBLANES - 2:SUBLANES - 2 + tm, :]
    cw = cw_ref[...]
    y = cw[0:1, :] * u2 + cw[1:2, :] * u1 + cw[2:3, :] * u
    yconv_ref[0] = (gate_b * y).astype(BF16)
    cq, s1q, s2q = cq_ref[0], s1q_ref[0], s2q_ref[0]
    ci, s1i, s2i = ci_ref[0], s1i_ref[0], s2i_ref[0]
    hq = ROPE_DIM // 2
    hi = IDX_ROPE_DIM // 2
    dq = dq_ref[0]
    scale = HEAD_DIM ** -0.5 * LOG2_E
    tq = TQ_DSA
    qr = [_rope_apply(dq[:, j * LANES:(j + 1) * LANES], cq, s1q, s2q, hq) * scale for j in range(ATT_WIDTH // LANES)]
    per_tile = LANES // HEAD_DIM
    for blk in range(tm // tq):
        rows = slice(blk * tq, (blk + 1) * tq)
        for h in range(DSA_HEADS):
            c0 = (h % per_tile) * HEAD_DIM
            q_ref[0, blk, h * tq:(h + 1) * tq, :] = qr[h // per_tile][rows, c0:c0 + HEAD_DIM].astype(BF16)
    lane = lax.broadcasted_iota(I32, (1, LANES), 1)
    is_k = lane < HEAD_DIM
    kvr = _rope_apply(dkv_ref[0], jnp.where(is_k, cq, 1.0), jnp.where(is_k, s1q, 0.0),
                      jnp.where(is_k, s2q, 0.0), hq)
    kt_ref[0, 0] = kvr.T[0:HEAD_DIM, :].astype(BF16)
    v_ref[0, 0] = kvr[:, HEAD_DIM:2 * HEAD_DIM].astype(BF16)

    def hi_lo(x):
        x_hi = x.astype(BF16).astype(F32)
        return x_hi, x - x_hi

    iq = iq_ref[0]
    iqr = [_rope_apply(iq[:, j * LANES:(j + 1) * LANES], ci, s1i, s2i, hi) for j in range(IDX_HEADS * IDX_DIM // LANES)]
    per_tile = LANES // IDX_DIM
    zero_q = jnp.zeros((tq, IDX_DIM), F32)
    for blk in range(tm // tq):
        rows = slice(blk * tq, (blk + 1) * tq)
        for h in range(IDX_HEADS):
            c0 = (h % per_tile) * IDX_DIM
            x_hi, x_lo = hi_lo(iqr[h // per_tile][rows, c0:c0 + IDX_DIM])
            iq3_ref[0, blk, h * tq:(h + 1) * tq, :] = jnp.concatenate([x_hi, x_lo, x_hi, zero_q], axis=1).astype(BF16)
    is_ik = lane < IDX_DIM
    is_iw = (lane >= IDX_DIM) & (lane < IDX_DIM + IDX_HEADS)
    ikw = _rope_apply(ikw_ref[0], jnp.where(is_ik, ci, 1.0), jnp.where(is_ik, s1i, 0.0),
                      jnp.where(is_ik, s2i, 0.0), hi)
    ikwr_ref[0] = ikw * jnp.where(is_iw, IDX_HEADS ** -0.5, 1.0)
    k_hi, k_lo = hi_lo(ikw[:, 0:IDX_DIM])
    ik3 = jnp.concatenate([k_hi, k_hi, k_lo, jnp.zeros((tm, IDX_DIM), F32)], axis=1)
    ik3_ref[0, 0] = ik3.T.astype(BF16)


def prep(proj, tabs_q, tabs_i, conv_w):
    B, S, _ = proj.shape
    tm = min(TM_PREP, S)
    hb = tm // SUBLANES

    def col(width, start):
        return pl.BlockSpec((1, tm, width), lambda b, i: (b, i, start // width))

    tab = pl.BlockSpec((1, tm, LANES), lambda b, i: (b, i, 0))
    halo = pl.BlockSpec((1, SUBLANES, 3 * CONV_WIDTH),
                        lambda b, i: (b, jnp.maximum(i * hb - 1, 0), COL_CONV // (3 * CONV_WIDTH)))

    def out(width, dtype):
        return jax.ShapeDtypeStruct((B, S, width), dtype), pl.BlockSpec((1, tm, width), lambda b, i: (b, i, 0))

    tq, tk, tka = TQ_DSA, TK_DSA, TKA_DSA
    assert tm == tk and tka % tm == 0 and S % tka == 0
    nqb, bq, r = S // tq, tm // tq, tka // tm
    outs = [out(CONV_WIDTH, BF16),
            (jax.ShapeDtypeStruct((B, nqb, DSA_HEADS * tq, HEAD_DIM), BF16),
             pl.BlockSpec((1, bq, DSA_HEADS * tq, HEAD_DIM), lambda b, i: (b, i, 0, 0))),
            (jax.ShapeDtypeStruct((B, S // tka, HEAD_DIM, tka), BF16),
             pl.BlockSpec((1, 1, HEAD_DIM, tm), lambda b, i: (b, i // r, 0, i % r))),
            (jax.ShapeDtypeStruct((B, S // tka, tka, HEAD_DIM), BF16),
             pl.BlockSpec((1, 1, tm, HEAD_DIM), lambda b, i: (b, i // r, i % r, 0))),
            (jax.ShapeDtypeStruct((B, nqb, IDX_HEADS * tq, LANES), BF16),
             pl.BlockSpec((1, bq, IDX_HEADS * tq, LANES), lambda b, i: (b, i, 0, 0))),
            (jax.ShapeDtypeStruct((B, S // tk, LANES, tk), BF16),
             pl.BlockSpec((1, 1, LANES, tk), lambda b, i: (b, i, 0, 0))),
            out(LANES, F32)]
    return pl.pallas_call(
        _prep_kernel,
        out_shape=tuple(o[0] for o in outs),
        grid=(B, S // tm),
        in_specs=[col(ATT_WIDTH, COL_DQ), col(3 * CONV_WIDTH, COL_CONV), halo,
                  col(IDX_HEADS * IDX_DIM, COL_IQ), col(LANES, COL_DKV), col(LANES, COL_IKW),
                  tab, tab, tab, tab, tab, tab,
                  pl.BlockSpec((CONV_K, CONV_WIDTH), lambda b, i: (0, 0))],
        out_specs=tuple(o[1] for o in outs),
        scratch_shapes=[pltpu.VMEM((tm + SUBLANES, CONV_WIDTH), F32)],
        compiler_params=_cparams(("parallel", "parallel")),
    )(proj, proj, proj, proj, proj, proj, *tabs_q, *tabs_i, conv_w)


def _sortable_key_np(v):
    b = np.array([v], dtype=np.float32).view(np.int32)[0]
    return int(b ^ ((b >> 31) & 0x7FFFFFFF))


KEY_LOWEST_FINITE = _sortable_key_np(-3.0e38)
KEY_NEG_INF = _sortable_key_np(-np.inf)
KEY_POS_INF = _sortable_key_np(np.inf)


def _key_of_float(x):
    bits = pltpu.bitcast(x, I32)
    return bits ^ ((bits >> 31) & 0x7FFFFFFF)


def _float_of_key(k):
    return pltpu.bitcast(k ^ ((k >> 31) & 0x7FFFFFFF), F32)


def _double_buffered_tiles(n, step):
    npairs = (n - 1) // 2

    def pair(u, carry):
        step(2 * u, 0, False)
        step(2 * u + 1, 1, False)
        return carry

    lax.fori_loop(0, npairs, pair, 0)
    r = 2 * npairs

    @pl.when(r == n - 2)
    def _():
        step(r, 0, False)
        step(r + 1, 1, True)

    @pl.when(r == n - 1)
    def _():
        step(r, 0, True)


def _dsa_kernel(iq3_ref, iw_ref, ik3_ref, q_ref, kt_ref, v_ref, tri_ref, o_ref,
                sc_scr, lg_scr, iwb_scr, cmax_scr, p_scr,
                m_scr, l_scr, a_scr, acc_scr, *, topk):
    tq = TQ_DSA
    tk = TK_DSA
    tka = TKA_DSA
    nsub = tk // LANES
    nsuba = tka // LANES
    qb = pl.program_id(1)
    nk = (qb + 1) * tq
    ntiles = (nk + tk - 1) // tk
    iw = iw_ref[0]
    for h in range(IDX_HEADS):
        iwb_scr[h] = jnp.broadcast_to(iw[:, IDX_DIM + h:IDX_DIM + h + 1], (tq, LANES))
    cmax_scr[...] = jnp.full(cmax_scr.shape, -jnp.inf, F32)

    def logits(t):
        return jnp.dot(iq3_ref[0, 0], ik3_ref[0, t], preferred_element_type=F32)

    lg_scr[0] = logits(0)

    def score_tile(t, slot, last):
        if last:
            row = lax.broadcasted_iota(I32, (tq, 1), 0)
            lim = ((qb * tq + row) // CHUNK + 1) * CHUNK
            lane = lax.broadcasted_iota(I32, (tq, LANES), 1)
        for j in range(nsub):
            js = slice(j * LANES, (j + 1) * LANES)
            sc = iwb_scr[0] * jnp.maximum(lg_scr[slot, 0:tq, js], 0.0)
            for h in range(1, IDX_HEADS):
                sc = sc + iwb_scr[h] * jnp.maximum(lg_scr[slot, h * tq:(h + 1) * tq, js], 0.0)
            if last:
                sc = jnp.where(t * tk + j * LANES + lane < lim, sc, -jnp.inf)
            cmax_scr[j % 2] = jnp.maximum(cmax_scr[j % 2], sc)
            sc_scr[t, :, js] = sc
        if not last:
            lg_scr[1 - slot] = logits(t + 1)

    _double_buffered_tiles(ntiles, score_tile)

    def count_ge(mid):
        vf = jnp.broadcast_to(_float_of_key(mid), (tq, LANES))

        def count_tile(t, cnt):
            for j in range(nsub):
                cnt = cnt + (sc_scr[t, :, j * LANES:(j + 1) * LANES] >= vf).astype(I32)
            return cnt

        cnt = lax.fori_loop(0, ntiles, count_tile, jnp.zeros((tq, LANES), I32))
        return jnp.sum(cnt, axis=1, keepdims=True)

    def bisect_step(c):
        it, lo, hi, above, exact = c
        int_mid = (lo | hi) - ((lo ^ hi) >> 1)
        val_mid = _key_of_float(0.5 * _float_of_key(lo) + 0.5 * _float_of_key(hi))
        val_mid = jnp.minimum(hi, jnp.maximum(val_mid, lo + 1))
        mid = jnp.where(it < BISECT_VALUE_STEPS, val_mid, int_mid)
        total = count_ge(mid)
        ge = total >= topk
        eq = total == topk
        return (it + 1, jnp.where(ge, mid, lo), jnp.where(eq, mid, jnp.where(ge, hi, mid - 1)),
                jnp.where(ge, above, total), exact | eq.astype(I32))

    def snap_step(c):
        it, lo, hi, above, exact = c
        lo_b = jnp.broadcast_to(_float_of_key(lo), (tq, LANES))
        hi_b = jnp.broadcast_to(_float_of_key(hi), (tq, LANES))

        def range_tile(t, mm):
            mn, mx = mm
            for j in range(nsub):
                x = sc_scr[t, :, j * LANES:(j + 1) * LANES]
                inside = (x >= lo_b) & (x <= hi_b)
                mn = jnp.minimum(mn, jnp.where(inside, x, jnp.inf))
                mx = jnp.maximum(mx, jnp.where(inside, x, -jnp.inf))
            return mn, mx

        mn, mx = lax.fori_loop(0, ntiles, range_tile,
                               (jnp.full((tq, LANES), jnp.inf, F32), jnp.full((tq, LANES), -jnp.inf, F32)))
        open_ = hi > lo
        lo_n = jnp.where(open_, _key_of_float(jnp.min(mn, axis=1, keepdims=True)), lo)
        hi_n = jnp.where(open_, _key_of_float(jnp.max(mx, axis=1, keepdims=True)), hi)
        return it + 1, lo_n, hi_n, above, exact

    def search_cond(c):
        return jnp.max((c[2] > c[1]).astype(I32)) > 0

    def search_body(c):
        it = c[0]
        is_snap = (it >= SNAP_FIRST_STEP) & ((it - SNAP_FIRST_STEP) % SNAP_EVERY == 0)
        return lax.cond(is_snap, snap_step, bisect_step, c)

    cm = cmax_scr[...]
    lo_f = jnp.min(jnp.minimum(cm[0], cm[1]), axis=1, keepdims=True)
    hi_f = jnp.max(jnp.maximum(cm[0], cm[1]), axis=1, keepdims=True)
    zeros_i = jnp.zeros((tq, 1), I32)
    _, lo, _, above, exact = lax.while_loop(
        search_cond, search_body, (jnp.int32(0), _key_of_float(lo_f), _key_of_float(hi_f), zeros_i, zeros_i))
    thr = _float_of_key(jnp.maximum(lo, KEY_LOWEST_FINITE))
    thr_b = jnp.broadcast_to(thr, (tq, LANES))
    ties_b = jnp.broadcast_to(jnp.where(exact != 0, ALL_TIES, (topk - above).astype(F32)), (tq, LANES))
    any_ties = jnp.max((exact == 0).astype(I32)) > 0

    m_scr[...] = jnp.full(m_scr.shape, NEG_BIG, F32)
    l_scr[...] = jnp.zeros(l_scr.shape, F32)
    acc_scr[...] = jnp.zeros(acc_scr.shape, F32)
    ratio = tka // tk

    ntiles_a = (ntiles + ratio - 1) // ratio

    @pl.when(ntiles % ratio != 0)
    def _():
        sc_scr[ntiles] = jnp.full((tq, tk), -jnp.inf, F32)

    sub_tiles = [(r, j) for r in range(ratio) for j in range(nsub)]

    def mask_plain(ta, carry):
        for r, j in sub_tiles:
            js = slice(j * LANES, (j + 1) * LANES)
            sc_scr[ratio * ta + r, :, js] = jnp.where(sc_scr[ratio * ta + r, :, js] >= thr_b, 0.0, NEG_BIG)
        return carry

    def mask_with_ties(ta, seen):
        xs = [sc_scr[ratio * ta + r, :, j * LANES:(j + 1) * LANES] for r, j in sub_tiles]
        ties = [jnp.where(x == thr_b, 1.0, 0.0) for x in xs]
        counts = [jnp.dot(tie.astype(BF16), tri_ref[...], preferred_element_type=F32) for tie in ties]
        for i, (r, j) in enumerate(sub_tiles):
            prefix, total = counts[i][:, 0:LANES], counts[i][:, LANES:2 * LANES]
            keep = jnp.where(seen + prefix <= ties_b, ties[i], 0.0)
            sc_scr[ratio * ta + r, :, j * LANES:(j + 1) * LANES] = jnp.where(
                xs[i] > thr_b, 0.0, jnp.where(keep > 0.5, 0.0, NEG_BIG))
            seen = seen + total
        return seen

    def run_mask_with_ties():
        lax.fori_loop(0, ntiles_a, mask_with_ties, jnp.zeros((tq, LANES), F32))
        return 0

    def run_mask_plain():
        lax.fori_loop(0, ntiles_a, mask_plain, 0)
        return 0

    lax.cond(any_ties, run_mask_with_ties, run_mask_plain)

    def attn_tile(t, carry):
        s = jnp.dot(q_ref[0, 0], kt_ref[0, t], preferred_element_type=F32)
        for h in range(DSA_HEADS):
            r0 = h * tq
            sb = [s[r0:r0 + tq, j * LANES:(j + 1) * LANES]
                  + sc_scr[ratio * t + j // nsub, :, (j % nsub) * LANES:(j % nsub + 1) * LANES]
                  for j in range(nsuba)]
            m_old = m_scr[h]
            m_new = jnp.maximum(m_old, jnp.max(functools.reduce(jnp.maximum, sb), axis=1, keepdims=True))
            a = jnp.exp2(m_old - m_new)
            ps = [jnp.exp2(x - m_new) for x in sb]
            l_scr[h] = a * l_scr[h] + functools.reduce(jnp.add, ps)
            m_scr[h] = m_new
            a_scr[h] = a
            for j in range(nsuba):
                p_scr[r0:r0 + tq, j * LANES:(j + 1) * LANES] = ps[j].astype(BF16)
        pv = jnp.dot(p_scr[...], v_ref[0, t], preferred_element_type=F32)
        for h in range(DSA_HEADS):
            acc_scr[h] = a_scr[h][:, 0:HEAD_DIM] * acc_scr[h] + pv[h * tq:(h + 1) * tq, :]
        return carry

    def attn_tile_stale_ref(t, jump):
        s = jnp.dot(q_ref[0, 0], kt_ref[0, t], preferred_element_type=F32)
        for h in range(DSA_HEADS):
            r0 = h * tq
            m_ref = m_scr[h]
            sb = [s[r0:r0 + tq, j * LANES:(j + 1) * LANES]
                  + sc_scr[ratio * t + j // nsub, :, (j % nsub) * LANES:(j % nsub + 1) * LANES]
                  for j in range(nsuba)]
            ps = [jnp.exp2(x - m_ref) for x in sb]
            m_tile = jnp.max(functools.reduce(jnp.maximum, sb), axis=1, keepdims=True)
            m_new = jnp.maximum(m_ref, m_tile)
            a = jnp.exp2(m_ref - m_new)
            l_scr[h] = (l_scr[h] + functools.reduce(jnp.add, ps)) * a
            m_scr[h] = m_new
            a_scr[h] = a
            jump = jnp.maximum(jump, m_tile - m_ref)
            for j in range(nsuba):
                p_scr[r0:r0 + tq, j * LANES:(j + 1) * LANES] = ps[j].astype(BF16)
        pv = jnp.dot(p_scr[...], v_ref[0, t], preferred_element_type=F32)
        for h in range(DSA_HEADS):
            acc_scr[h] = (acc_scr[h] + pv[h * tq:(h + 1) * tq, :]) * a_scr[h][:, 0:HEAD_DIM]
        return jump

    def init_softmax_state():
        m_scr[...] = jnp.full(m_scr.shape, NEG_BIG, F32)
        l_scr[...] = jnp.zeros(l_scr.shape, F32)
        acc_scr[...] = jnp.zeros(acc_scr.shape, F32)

    ntiles_a = (ntiles + ratio - 1) // ratio
    attn_tile(0, 0)
    jump = lax.fori_loop(1, ntiles_a, attn_tile_stale_ref, jnp.zeros((tq, LANES), F32))
    safe = jnp.max(jump) <= MAX_STALE_JUMP

    @pl.when(jnp.logical_not(safe))
    def _():
        init_softmax_state()
        lax.fori_loop(0, ntiles_a, attn_tile, 0)

    o_ref[0] = jnp.concatenate([acc_scr[h] / jnp.sum(l_scr[h], axis=1, keepdims=True) for h in range(DSA_HEADS)],
                               axis=1).astype(BF16)


def dsa_attention(qst, kt, v, iq3, ik3, ikw):
    B, nqb = qst.shape[:2]
    tq, tk, tka = TQ_DSA, TK_DSA, TKA_DSA
    S = nqb * tq
    nkt, nka = S // tk, S // tka
    topk = min(TOPK_MAX, S // 4)
    per_batch = pl.Buffered(1)
    ones = np.ones((LANES, LANES), np.float32)
    tri = jnp.asarray(np.concatenate([np.triu(ones), ones], axis=1), BF16)
    return pl.pallas_call(
        functools.partial(_dsa_kernel, topk=topk),
        out_shape=jax.ShapeDtypeStruct((B, S, ATT_WIDTH), BF16),
        grid=(B, nqb),
        in_specs=[
            pl.BlockSpec((1, 1, IDX_HEADS * tq, LANES), lambda b, i: (b, i, 0, 0)),
            pl.BlockSpec((1, tq, LANES), lambda b, i: (b, i, 0)),
            pl.BlockSpec((1, nkt, LANES, tk), lambda b, i: (b, 0, 0, 0), pipeline_mode=per_batch),
            pl.BlockSpec((1, 1, DSA_HEADS * tq, HEAD_DIM), lambda b, i: (b, i, 0, 0)),
            pl.BlockSpec((1, nka, HEAD_DIM, tka), lambda b, i: (b, 0, 0, 0), pipeline_mode=per_batch),
            pl.BlockSpec((1, nka, tka, HEAD_DIM), lambda b, i: (b, 0, 0, 0), pipeline_mode=per_batch),
            pl.BlockSpec((LANES, 2 * LANES), lambda b, i: (0, 0)),
        ],
        out_specs=pl.BlockSpec((1, tq, ATT_WIDTH), lambda b, i: (b, i, 0)),
        scratch_shapes=[
            pltpu.VMEM((nkt, tq, tk), F32),
            pltpu.VMEM((2, IDX_HEADS * tq, tk), F32),
            pltpu.VMEM((IDX_HEADS, tq, LANES), F32),
            pltpu.VMEM((2, tq, LANES), F32),
            pltpu.VMEM((DSA_HEADS * tq, tka), BF16),
            pltpu.VMEM((DSA_HEADS, tq, LANES), F32),
            pltpu.VMEM((DSA_HEADS, tq, LANES), F32),
            pltpu.VMEM((DSA_HEADS, tq, LANES), F32),
            pltpu.VMEM((DSA_HEADS, tq, HEAD_DIM), F32),
        ],
        compiler_params=_cparams(("parallel", "arbitrary")),
    )(iq3, ikw, ik3, qst, kt, v, tri)


def _band_kernel(q_ref, k0_ref, k1_ref, k2_ref, v0_ref, v1_ref, v2_ref, bias_ref, o_ref):
    tq = TQ_BAND
    i = pl.program_id(1)
    q = (q_ref[0] * (HEAD_DIM ** -0.5)).astype(BF16)
    ks = [r[0].astype(BF16) for r in (k0_ref, k1_ref, k2_ref)]
    vs = [r[0].astype(BF16) for r in (v0_ref, v1_ref, v2_ref)]
    nt = len(ks)
    for h in range(CHK_HEADS):
        hs = slice(h * HEAD_DIM, (h + 1) * HEAD_DIM)
        parts = []
        for j in range(nt):
            s = lax.dot_general(q[:, hs], ks[j][:, hs], (((1,), (1,)), ((), ())), preferred_element_type=F32)
            s = s + bias_ref[h, :, j * tq:(j + 1) * tq]
            if j < nt - 1:
                s = jnp.where(i + j >= nt - 1, s, NEG_BIG)
            parts.append(s)
        m = functools.reduce(jnp.maximum, [jnp.max(s, axis=-1, keepdims=True) for s in parts])
        ps = [jnp.exp(s - m) for s in parts]
        l = functools.reduce(jnp.add, [jnp.sum(p, axis=-1, keepdims=True) for p in ps])
        o = functools.reduce(jnp.add, [jnp.dot(p.astype(BF16), vs[j][:, hs], preferred_element_type=F32)
                                        for j, p in enumerate(ps)])
        o_ref[0, :, hs] = (o / l).astype(BF16)


def _band_bias(rel_bias):
    tq = TQ_BAND
    nt = BAND_CHUNKS * CHUNK // tq + 1
    win = nt * tq
    band = (BAND_CHUNKS + 1) * CHUNK
    H = rel_bias.shape[0]
    n = band + CHUNK - 1
    g = jnp.concatenate([rel_bias, jnp.broadcast_to(rel_bias[:, -1:], (H, n - rel_bias.shape[1]))], axis=1)[:, ::-1]
    hankel = jnp.tile(g, (1, CHUNK + 1))[:, :CHUNK * (n + 1)].reshape(H, CHUNK, n + 1)[:, :, :band]
    w = hankel[:, ::-1, :]
    rows = [jnp.pad(w, ((0, 0), (0, 0), (a * CHUNK, win - band - a * CHUNK)), constant_values=NEG_BIG)
            for a in range(tq // CHUNK)]
    return jnp.concatenate(rows, axis=1).astype(F32)


def band_attention(proj, rel_bias):
    B, S, _ = proj.shape
    tq = TQ_BAND
    nt = BAND_CHUNKS * CHUNK // tq + 1
    bias = _band_bias(rel_bias)

    def kspec(col, j):
        return pl.BlockSpec((1, tq, ATT_WIDTH), lambda b, i: (b, jnp.maximum(i + j - (nt - 1), 0), col // ATT_WIDTH))

    return pl.pallas_call(
        _band_kernel,
        out_shape=jax.ShapeDtypeStruct((B, S, ATT_WIDTH), BF16),
        grid=(B, S // tq),
        in_specs=[pl.BlockSpec((1, tq, ATT_WIDTH), lambda b, i: (b, i, COL_CQ // ATT_WIDTH))]
                 + [kspec(COL_CK, j) for j in range(nt)] + [kspec(COL_CV, j) for j in range(nt)]
                 + [pl.BlockSpec((CHK_HEADS, tq, nt * tq), lambda b, i: (0, 0, 0))],
        out_specs=pl.BlockSpec((1, tq, ATT_WIDTH), lambda b, i: (b, i, 0)),
        compiler_params=_cparams(("parallel", "parallel")),
    )(proj, proj, proj, proj, proj, proj, proj, bias)


def _outproj_kernel(yc_ref, yd_ref, yb_ref, x_ref, w_ref, g_ref, b_ref, o_ref):
    c0, c1 = CONV_WIDTH, CONV_WIDTH + ATT_WIDTH
    mix = jnp.dot(yc_ref[...], w_ref[0:c0, :], preferred_element_type=F32)
    mix = mix + jnp.dot(yd_ref[...], w_ref[c0:c1, :], preferred_element_type=F32)
    mix = mix + jnp.dot(yb_ref[...], w_ref[c1:D_MODEL, :], preferred_element_type=F32)
    o_ref[...] = _layer_norm(ALPHA * x_ref[...] + mix, g_ref[...], b_ref[...])


def outproj_ln(yc, yd, yb, x, w, g, b):
    T = x.shape[0]
    tm = min(TM_OUT, T)
    rows = lambda width: pl.BlockSpec((tm, width), lambda i: (i, 0))
    vec = pl.BlockSpec((1, D_MODEL), lambda i: (0, 0))
    return pl.pallas_call(
        _outproj_kernel,
        out_shape=jax.ShapeDtypeStruct((T, D_MODEL), F32),
        grid=(T // tm,),
        in_specs=[rows(CONV_WIDTH), rows(ATT_WIDTH), rows(ATT_WIDTH), rows(D_MODEL),
                  pl.BlockSpec((D_MODEL, D_MODEL), lambda i: (0, 0)), vec, vec],
        out_specs=rows(D_MODEL),
        compiler_params=_cparams(("parallel",)),
    )(yc, yd, yb, x, w, g, b)


def _cross_kernel(x_ref, wq_ref, k_ref, v_ref, wo_ref, g_ref, b_ref, o_ref):
    x = x_ref[0]
    q = jnp.dot(x.astype(BF16), wq_ref[...], preferred_element_type=F32)
    q = (q * (MEM_HEAD_DIM ** -0.5)).astype(BF16)
    k = k_ref[0]
    v = v_ref[0]
    outs = []
    for h in range(MEM_HEADS):
        hs = slice(h * MEM_HEAD_DIM, (h + 1) * MEM_HEAD_DIM)
        s = lax.dot_general(q[:, hs], k[:, hs], (((1,), (1,)), ((), ())), preferred_element_type=F32)
        p = jnp.exp(s - jnp.max(s, axis=-1, keepdims=True))
        p = p / jnp.sum(p, axis=-1, keepdims=True)
        outs.append(jnp.dot(p.astype(BF16), v[:, hs], preferred_element_type=F32).astype(BF16))
    o = jnp.concatenate(outs, axis=-1)
    cross = jnp.dot(o, wo_ref[...], preferred_element_type=F32)
    o_ref[0] = _layer_norm(ALPHA * x + cross, g_ref[...], b_ref[...])


def cross_ln(x, k, v, wq, wo, g, b):
    B, S, _ = x.shape
    M = k.shape[1]
    tm = min(TM_CROSS, S)
    full = lambda shape: pl.BlockSpec(shape, lambda bb, i: (0,) * len(shape))
    return pl.pallas_call(
        _cross_kernel,
        out_shape=jax.ShapeDtypeStruct((B, S, D_MODEL), F32),
        grid=(B, S // tm),
        in_specs=[pl.BlockSpec((1, tm, D_MODEL), lambda bb, i: (bb, i, 0)),
                  full((D_MODEL, D_MODEL)),
                  pl.BlockSpec((1, M, D_MODEL), lambda bb, i: (bb, 0, 0)),
                  pl.BlockSpec((1, M, D_MODEL), lambda bb, i: (bb, 0, 0)),
                  full((D_MODEL, D_MODEL)), full((1, D_MODEL)), full((1, D_MODEL))],
        out_specs=pl.BlockSpec((1, tm, D_MODEL), lambda bb, i: (bb, i, 0)),
        compiler_params=_cparams(("parallel", "parallel")),
    )(x, wq, k, v, wo, g, b)


def _mlp_kernel(x_ref, w1_ref, w2_ref, g_ref, b_ref, o_ref, acc_ref):
    j = pl.program_id(1)

    @pl.when(j == 0)
    def _():
        acc_ref[...] = jnp.zeros(acc_ref.shape, F32)

    h = jnp.dot(x_ref[...].astype(BF16), w1_ref[...], preferred_element_type=F32)
    h = jnp.square(jnp.maximum(h, 0.0)).astype(BF16)
    acc_ref[...] += jnp.dot(h, w2_ref[...], preferred_element_type=F32)

    @pl.when(j == pl.num_programs(1) - 1)
    def _():
        o_ref[...] = _layer_norm(ALPHA * x_ref[...] + acc_ref[...], g_ref[...], b_ref[...])


def mlp_ln(x, w1, w2, g, b):
    T = x.shape[0]
    tm = min(TM_MLP, T)
    tf = TF_MLP
    vec = pl.BlockSpec((1, D_MODEL), lambda i, j: (0, 0))
    return pl.pallas_call(
        _mlp_kernel,
        out_shape=jax.ShapeDtypeStruct((T, D_MODEL), F32),
        grid=(T // tm, D_FF // tf),
        in_specs=[pl.BlockSpec((tm, D_MODEL), lambda i, j: (i, 0)),
                  pl.BlockSpec((D_MODEL, tf), lambda i, j: (0, j)),
                  pl.BlockSpec((tf, D_MODEL), lambda i, j: (j, 0)), vec, vec],
        out_specs=pl.BlockSpec((tm, D_MODEL), lambda i, j: (i, 0)),
        scratch_shapes=[pltpu.VMEM((tm, D_MODEL), F32)],
        compiler_params=_cparams(("parallel", "arbitrary")),
    )(x, w1, w2, g, b)


def _permute_w_in(w_in):
    sizes = (CONV_WIDTH, CONV_WIDTH, CONV_WIDTH, ATT_WIDTH, HEAD_DIM, HEAD_DIM,
             IDX_HEADS * IDX_DIM, IDX_DIM, IDX_HEADS, ATT_WIDTH, ATT_WIDTH, ATT_WIDTH)
    offs = np.concatenate([[0], np.cumsum(sizes)])
    seg = [w_in[:, offs[k]:offs[k + 1]] for k in range(len(sizes))]
    cb, cc, ch, dq, dk, dv, iq, ik, iw, cq, ck, cv = seg
    pad = jnp.zeros((w_in.shape[0], LANES - IDX_DIM - IDX_HEADS), w_in.dtype)
    return jnp.concatenate([cq, ck, cv, dq, cb, cc, ch, iq, dk, dv, ik, iw, pad], axis=1).astype(BF16)


@jax.jit
def kernel(x, mem, positions, w_in, conv_w, rel_bias, w_mix_out, ln1_g, ln1_b,
           w_mq, w_mkv, w_mo, ln2_g, ln2_b, w_ff1, w_ff2, ln3_g, ln3_b):
    B, S, D = x.shape
    T = B * S
    M = mem.shape[1]
    tabs_q = rope_tables(positions, HEAD_DIM, ROPE_DIM)
    tabs_i = rope_tables(positions, IDX_DIM, IDX_ROPE_DIM)
    mem2 = mem.reshape(B * M, D)
    xf = x.reshape(T, D)
    vec = lambda a: a.reshape(1, D)
    for l in range(w_in.shape[0]):
        proj = matmul(xf, _permute_w_in(w_in[l]), min(TM_PROJ, T)).reshape(B, S, PROJ_W)
        yconv, qst, kt, v, iq3, ik3, ikw = prep(proj, tabs_q, tabs_i, conv_w[l])
        ydsa = dsa_attention(qst, kt, v, iq3, ik3, ikw)
        yband = band_attention(proj, rel_bias[l])
        xf = outproj_ln(yconv.reshape(T, -1), ydsa.reshape(T, -1), yband.reshape(T, -1), xf,
                        w_mix_out[l].astype(BF16), vec(ln1_g[l]), vec(ln1_b[l]))
        kvm = matmul(mem2, w_mkv[l].astype(BF16), min(512, B * M)).reshape(B, M, 2 * D)
        xf = cross_ln(xf.reshape(B, S, D), kvm[..., :D].astype(BF16), kvm[..., D:].astype(BF16),
                      w_mq[l].astype(BF16), w_mo[l].astype(BF16), vec(ln2_g[l]), vec(ln2_b[l])).reshape(T, D)
        xf = mlp_ln(xf, w_ff1[l].astype(BF16), w_ff2[l].astype(BF16), vec(ln3_g[l]), vec(ln3_b[l]))
    return xf.reshape(B, S, D)
```

```python
import functools

import numpy as np
import jax
import jax.numpy as jnp
from jax import lax
from jax.experimental import pallas as pl
from jax.experimental.pallas import tpu as pltpu

F32 = jnp.float32
BF16 = jnp.bfloat16
I32 = jnp.int32

D_MODEL = 1024
CHUNK = 64
HEAD_DIM = 64
CONV_WIDTH = 256
CONV_K = 3
ATT_WIDTH = 384
DSA_HEADS = 6
CHK_HEADS = 6
ROPE_DIM = 16
ROPE_THETA = 500000.0
IDX_HEADS = 8
IDX_DIM = 32
IDX_ROPE_DIM = 8
TOPK_MAX = 256
BAND_CHUNKS = 8
REL_CLIP = 128
MEM_HEADS = 4
MEM_HEAD_DIM = 256
D_FF = 4096
LN_EPS = 1e-5
DEPTH = 4
ALPHA = (2.0 * DEPTH) ** 0.25

LANES = 128
SUBLANES = 8
VMEM_LIMIT = 52 * 1024 * 1024

COL_CQ, COL_CK, COL_CV = 0, 384, 768
COL_DQ = 1152
COL_CONV = 1536
COL_IQ = 2304
COL_DKV = 2560
COL_IKW = 2688
PROJ_W = 2816

TM_PROJ = 512
TM_PREP = 512
TQ_DSA = 128
TK_DSA = 512
TKA_DSA = 1024
SCORE_UNROLL = 4
COUNT_UNROLL = 4
ATTN_UNROLL = 4
TQ_BAND = 256
TM_OUT = 512
TM_CROSS = 512
TM_MLP = 1024
TF_MLP = 1024

LOG2_E = 1.4426950408889634
NEG_BIG = -1e30
ALL_TIES = 1e9
MAX_STALE_JUMP = 100.0
BISECT_UNCHECKED_STEPS = 13
BISECT_VALUE_STEPS = 12
SNAP_FIRST_STEP = 14
SNAP_EVERY = 6


def _cparams(sem):
    return pltpu.CompilerParams(dimension_semantics=sem, vmem_limit_bytes=VMEM_LIMIT)


def _layer_norm(y, g, b):
    mu = jnp.mean(y, axis=-1, keepdims=True)
    d = y - mu
    var = jnp.mean(d * d, axis=-1, keepdims=True)
    return d * lax.rsqrt(var + LN_EPS) * g + b


def _rope_table_kernel(pos_ref, freq_ref, mc_ref, m1_ref, m2_ref, c_ref, s1_ref, s2_ref):
    ang = pos_ref[0].astype(F32) * freq_ref[...]
    c = jnp.cos(ang)
    s = jnp.sin(ang)
    mc = mc_ref[...]
    c_ref[0] = c * mc + (1.0 - mc)
    s1_ref[0] = s * m1_ref[...]
    s2_ref[0] = -(s * m2_ref[...])


def _rope_lane_consts(head_dim, rot_dim):
    half = rot_dim // 2
    inv_freq = np.power(np.float32(ROPE_THETA), -np.arange(half, dtype=np.float32) / np.float32(half)).astype(np.float32)
    lane = np.arange(LANES)
    dd = lane % head_dim
    freq = np.where(dd < rot_dim, inv_freq[lane % half], 0.0).astype(np.float32)
    mc = (dd < rot_dim).astype(np.float32)
    m1 = ((dd >= half) & (dd < rot_dim)).astype(np.float32)
    m2 = (dd < half).astype(np.float32)
    return [jnp.asarray(a.reshape(1, LANES)) for a in (freq, mc, m1, m2)]


def rope_tables(positions, head_dim, rot_dim):
    B, S = positions.shape
    tm = min(1024, S)
    consts = _rope_lane_consts(head_dim, rot_dim)
    row = pl.BlockSpec((1, LANES), lambda b, i: (0, 0))
    tab = pl.BlockSpec((1, tm, LANES), lambda b, i: (b, i, 0))
    out = jax.ShapeDtypeStruct((B, S, LANES), F32)
    return pl.pallas_call(
        _rope_table_kernel,
        out_shape=(out, out, out),
        grid=(B, S // tm),
        in_specs=[pl.BlockSpec((1, tm, 1), lambda b, i: (b, i, 0)), row, row, row, row],
        out_specs=(tab, tab, tab),
        compiler_params=_cparams(("parallel", "parallel")),
    )(positions.reshape(B, S, 1), *consts)


def _mm_kernel(x_ref, w_ref, o_ref):
    o_ref[...] = jnp.dot(x_ref[...].astype(BF16), w_ref[...], preferred_element_type=F32)


def matmul(x, w, tm):
    M, K = x.shape
    N = w.shape[1]
    return pl.pallas_call(
        _mm_kernel,
        out_shape=jax.ShapeDtypeStruct((M, N), F32),
        grid=(M // tm,),
        in_specs=[pl.BlockSpec((tm, K), lambda i: (i, 0)), pl.BlockSpec((K, N), lambda i: (0, 0))],
        out_specs=pl.BlockSpec((tm, N), lambda i: (i, 0)),
        compiler_params=_cparams(("parallel",)),
    )(x, w)


def _rope_apply(x, c, s1, s2, half):
    return x * c + pltpu.roll(x, half, 1) * s1 + pltpu.roll(x, LANES - half, 1) * s2


def _prep_kernel(dq_ref, conv_ref, halo_ref, iq_ref, dkv_ref, ikw_ref,
                 cq_ref, s1q_ref, s2q_ref, ci_ref, s1i_ref, s2i_ref, cw_ref,
                 yconv_ref, q_ref, kt_ref, v_ref, iq3_ref, ik3_ref, ikwr_ref, u_scr):
    tm = dq_ref.shape[1]
    i = pl.program_id(1)
    cv = conv_ref[0]
    gate_b = cv[:, 0:CONV_WIDTH]
    u = cv[:, CONV_WIDTH:2 * CONV_WIDTH] * cv[:, 2 * CONV_WIDTH:3 * CONV_WIDTH]
    hv = halo_ref[0]
    hu = hv[:, CONV_WIDTH:2 * CONV_WIDTH] * hv[:, 2 * CONV_WIDTH:3 * CONV_WIDTH]
    hu = jnp.where(i > 0, hu, 0.0)
    u_scr[0:SUBLANES, :] = hu
    u_scr[SUBLANES:SUBLANES + tm, :] = u
    u1 = u_scr[SUBLANES - 1:SUBLANES - 1 + tm, :]
    u2 = u_scr[SUBLANES - 2:SUBLANES - 2 + tm, :]
    cw = cw_ref[...]
    y = cw[0:1, :] * u2 + cw[1:2, :] * u1 + cw[2:3, :] * u
    yconv_ref[0] = (gate_b * y).astype(BF16)
    cq, s1q, s2q = cq_ref[0], s1q_ref[0], s2q_ref[0]
    ci, s1i, s2i = ci_ref[0], s1i_ref[0], s2i_ref[0]
    hq = ROPE_DIM // 2
    hi = IDX_ROPE_DIM // 2
    dq = dq_ref[0]
    scale = HEAD_DIM ** -0.5 * LOG2_E
    tq = TQ_DSA
    qr = [_rope_apply(dq[:, j * LANES:(j + 1) * LANES], cq, s1q, s2q, hq) * scale for j in range(ATT_WIDTH // LANES)]
    per_tile = LANES // HEAD_DIM
    for blk in range(tm // tq):
        rows = slice(blk * tq, (blk + 1) * tq)
        for h in range(DSA_HEADS):
            c0 = (h % per_tile) * HEAD_DIM
            q_ref[0, blk, h * tq:(h + 1) * tq, :] = qr[h // per_tile][rows, c0:c0 + HEAD_DIM].astype(BF16)
    lane = lax.broadcasted_iota(I32, (1, LANES), 1)
    is_k = lane < HEAD_DIM
    kvr = _rope_apply(dkv_ref[0], jnp.where(is_k, cq, 1.0), jnp.where(is_k, s1q, 0.0),
                      jnp.where(is_k, s2q, 0.0), hq)
    kt_ref[0, 0] = kvr.T[0:HEAD_DIM, :].astype(BF16)
    v_ref[0, 0] = kvr[:, HEAD_DIM:2 * HEAD_DIM].astype(BF16)

    def hi_lo(x):
        x_hi = x.astype(BF16).astype(F32)
        return x_hi, x - x_hi

    iq = iq_ref[0]
    iqr = [_rope_apply(iq[:, j * LANES:(j + 1) * LANES], ci, s1i, s2i, hi) for j in range(IDX_HEADS * IDX_DIM // LANES)]
    per_tile = LANES // IDX_DIM
    zero_q = jnp.zeros((tq, IDX_DIM), F32)
    for blk in range(tm // tq):
        rows = slice(blk * tq, (blk + 1) * tq)
        for h in range(IDX_HEADS):
            c0 = (h % per_tile) * IDX_DIM
            x_hi, x_lo = hi_lo(iqr[h // per_tile][rows, c0:c0 + IDX_DIM])
            iq3_ref[0, blk, h * tq:(h + 1) * tq, :] = jnp.concatenate([x_hi, x_lo, x_hi, zero_q], axis=1).astype(BF16)
    is_ik = lane < IDX_DIM
    is_iw = (lane >= IDX_DIM) & (lane < IDX_DIM + IDX_HEADS)
    ikw = _rope_apply(ikw_ref[0], jnp.where(is_ik, ci, 1.0), jnp.where(is_ik, s1i, 0.0),
                      jnp.where(is_ik, s2i, 0.0), hi)
    ikwr_ref[0] = ikw * jnp.where(is_iw, IDX_HEADS ** -0.5, 1.0)
    k_hi, k_lo = hi_lo(ikw[:, 0:IDX_DIM])
    ik3 = jnp.concatenate([k_hi, k_hi, k_lo, jnp.zeros((tm, IDX_DIM), F32)], axis=1)
    ik3_ref[0, 0] = ik3.T.astype(BF16)


def prep(proj, tabs_q, tabs_i, conv_w):
    B, S, _ = proj.shape
    tm = min(TM_PREP, S)
    hb = tm // SUBLANES

    def col(width, start):
        return pl.BlockSpec((1, tm, width), lambda b, i: (b, i, start // width))

    tab = pl.BlockSpec((1, tm, LANES), lambda b, i: (b, i, 0))
    halo = pl.BlockSpec((1, SUBLANES, 3 * CONV_WIDTH),
                        lambda b, i: (b, jnp.maximum(i * hb - 1, 0), COL_CONV // (3 * CONV_WIDTH)))

    def out(width, dtype):
        return jax.ShapeDtypeStruct((B, S, width), dtype), pl.BlockSpec((1, tm, width), lambda b, i: (b, i, 0))

    tq, tk, tka = TQ_DSA, TK_DSA, TKA_DSA
    assert tm == tk and tka % tm == 0 and S % tka == 0
    nqb, bq, r = S // tq, tm // tq, tka // tm
    outs = [out(CONV_WIDTH, BF16),
            (jax.ShapeDtypeStruct((B, nqb, DSA_HEADS * tq, HEAD_DIM), BF16),
             pl.BlockSpec((1, bq, DSA_HEADS * tq, HEAD_DIM), lambda b, i: (b, i, 0, 0))),
            (jax.ShapeDtypeStruct((B, S // tka, HEAD_DIM, tka), BF16),
             pl.BlockSpec((1, 1, HEAD_DIM, tm), lambda b, i: (b, i // r, 0, i % r))),
            (jax.ShapeDtypeStruct((B, S // tka, tka, HEAD_DIM), BF16),
             pl.BlockSpec((1, 1, tm, HEAD_DIM), lambda b, i: (b, i // r, i % r, 0))),
            (jax.ShapeDtypeStruct((B, nqb, IDX_HEADS * tq, LANES), BF16),
             pl.BlockSpec((1, bq, IDX_HEADS * tq, LANES), lambda b, i: (b, i, 0, 0))),
            (jax.ShapeDtypeStruct((B, S // tk, LANES, tk), BF16),
             pl.BlockSpec((1, 1, LANES, tk), lambda b, i: (b, i, 0, 0))),
            out(LANES, F32)]
    return pl.pallas_call(
        _prep_kernel,
        out_shape=tuple(o[0] for o in outs),
        grid=(B, S // tm),
        in_specs=[col(ATT_WIDTH, COL_DQ), col(3 * CONV_WIDTH, COL_CONV), halo,
                  col(IDX_HEADS * IDX_DIM, COL_IQ), col(LANES, COL_DKV), col(LANES, COL_IKW),
                  tab, tab, tab, tab, tab, tab,
                  pl.BlockSpec((CONV_K, CONV_WIDTH), lambda b, i: (0, 0))],
        out_specs=tuple(o[1] for o in outs),
        scratch_shapes=[pltpu.VMEM((tm + SUBLANES, CONV_WIDTH), F32)],
        compiler_params=_cparams(("parallel", "parallel")),
    )(proj, proj, proj, proj, proj, proj, *tabs_q, *tabs_i, conv_w)


def _sortable_key_np(v):
    b = np.array([v], dtype=np.float32).view(np.int32)[0]
    return int(b ^ ((b >> 31) & 0x7FFFFFFF))


KEY_LOWEST_FINITE = _sortable_key_np(-3.0e38)
KEY_NEG_INF = _sortable_key_np(-np.inf)
KEY_POS_INF = _sortable_key_np(np.inf)


def _key_of_float(x):
    bits = pltpu.bitcast(x, I32)
    return bits ^ ((bits >> 31) & 0x7FFFFFFF)


def _float_of_key(k):
    return pltpu.bitcast(k ^ ((k >> 31) & 0x7FFFFFFF), F32)


def _unrolled_tiles(start, stop, unroll, step, carry):
    ngroups = (stop - start) // unroll

    def group(g, c):
        for u in range(unroll):
            c = step(start + g * unroll + u, c)
        return c

    carry = lax.fori_loop(0, ngroups, group, carry)
    for u in range(unroll - 1):
        t = start + ngroups * unroll + u
        carry = lax.cond(t < stop, lambda c: step(t, c), lambda c: c, carry)
    return carry


def _dsa_kernel(iq3_ref, iw_ref, ik3_ref, q_ref, kt_ref, v_ref, tri_ref, o_ref,
                sc_scr, iwb_scr, cmax_scr, p_scr,
                m_scr, l_scr, a_scr, acc_scr, *, topk):
    tq = TQ_DSA
    tk = TK_DSA
    tka = TKA_DSA
    nsub = tk // LANES
    nsuba = tka // LANES
    qb = pl.program_id(1)
    nk = (qb + 1) * tq
    ntiles = (nk + tk - 1) // tk
    iw = iw_ref[0]
    for h in range(IDX_HEADS):
        iwb_scr[h] = jnp.broadcast_to(iw[:, IDX_DIM + h:IDX_DIM + h + 1], (tq, LANES))
    cmax_scr[...] = jnp.full(cmax_scr.shape, -jnp.inf, F32)

    def score_tile(t, last):
        if last:
            row = lax.broadcasted_iota(I32, (tq, 1), 0)
            lim = ((qb * tq + row) // CHUNK + 1) * CHUNK
            lane = lax.broadcasted_iota(I32, (tq, LANES), 1)
        lg = jnp.dot(iq3_ref[0, 0], ik3_ref[0, t], preferred_element_type=F32)
        for j in range(nsub):
            js = slice(j * LANES, (j + 1) * LANES)
            sc = iwb_scr[0] * jnp.maximum(lg[0:tq, js], 0.0)
            for h in range(1, IDX_HEADS):
                sc = sc + iwb_scr[h] * jnp.maximum(lg[h * tq:(h + 1) * tq, js], 0.0)
            if last:
                sc = jnp.where(t * tk + j * LANES + lane < lim, sc, -jnp.inf)
            cmax_scr[j % 2] = jnp.maximum(cmax_scr[j % 2], sc)
            sc_scr[t, :, js] = sc

    def score_body(t, carry):
        score_tile(t, False)
        return carry

    _unrolled_tiles(0, ntiles - 1, SCORE_UNROLL, score_body, 0)
    score_tile(ntiles - 1, True)

    def count_ge(mid):
        vf = jnp.broadcast_to(_float_of_key(mid), (tq, LANES))

        def count_tile(t, cnt):
            for j in range(nsub):
                cnt = cnt + (sc_scr[t, :, j * LANES:(j + 1) * LANES] >= vf).astype(I32)
            return cnt

        cnt = _unrolled_tiles(0, ntiles, COUNT_UNROLL, count_tile, jnp.zeros((tq, LANES), I32))
        return jnp.sum(cnt, axis=1, keepdims=True)

    def bisect_step(c):
        it, lo, hi, above, exact = c
        int_mid = (lo | hi) - ((lo ^ hi) >> 1)
        val_mid = _key_of_float(0.5 * _float_of_key(lo) + 0.5 * _float_of_key(hi))
        val_mid = jnp.minimum(hi, jnp.maximum(val_mid, lo + 1))
        mid = jnp.where(it < BISECT_VALUE_STEPS, val_mid, int_mid)
        total = count_ge(mid)
        ge = total >= topk
        eq = total == topk
        return (it + 1, jnp.where(ge, mid, lo), jnp.where(eq, mid, jnp.where(ge, hi, mid - 1)),
                jnp.where(ge, above, total), exact | eq.astype(I32))

    def snap_step(c):
        it, lo, hi, above, exact = c
        lo_b = jnp.broadcast_to(_float_of_key(lo), (tq, LANES))
        hi_b = jnp.broadcast_to(_float_of_key(hi), (tq, LANES))

        def range_tile(t, mm):
            mn, mx = mm
            for j in range(nsub):
                x = sc_scr[t, :, j * LANES:(j + 1) * LANES]
                inside = (x >= lo_b) & (x <= hi_b)
                mn = jnp.minimum(mn, jnp.where(inside, x, jnp.inf))
                mx = jnp.maximum(mx, jnp.where(inside, x, -jnp.inf))
            return mn, mx

        mn, mx = lax.fori_loop(0, ntiles, range_tile,
                               (jnp.full((tq, LANES), jnp.inf, F32), jnp.full((tq, LANES), -jnp.inf, F32)))
        open_ = hi > lo
        lo_n = jnp.where(open_, _key_of_float(jnp.min(mn, axis=1, keepdims=True)), lo)
        hi_n = jnp.where(open_, _key_of_float(jnp.max(mx, axis=1, keepdims=True)), hi)
        return it + 1, lo_n, hi_n, above, exact

    def search_cond(c):
        return jnp.max((c[2] > c[1]).astype(I32)) > 0

    def search_body(c):
        it = c[0]
        is_snap = (it >= SNAP_FIRST_STEP) & ((it - SNAP_FIRST_STEP) % SNAP_EVERY == 0)
        return lax.cond(is_snap, snap_step, bisect_step, c)

    cm = cmax_scr[...]
    lo_f = jnp.min(jnp.minimum(cm[0], cm[1]), axis=1, keepdims=True)
    hi_f = jnp.max(jnp.maximum(cm[0], cm[1]), axis=1, keepdims=True)
    zeros_i = jnp.zeros((tq, 1), I32)
    state = (jnp.int32(0), _key_of_float(lo_f), _key_of_float(hi_f), zeros_i, zeros_i)
    state = lax.fori_loop(0, BISECT_UNCHECKED_STEPS, lambda i, c: bisect_step(c), state)
    _, lo, _, above, exact = lax.while_loop(search_cond, search_body, state)
    thr = _float_of_key(jnp.maximum(lo, KEY_LOWEST_FINITE))
    thr_b = jnp.broadcast_to(thr, (tq, LANES))
    ties_b = jnp.broadcast_to(jnp.where(exact != 0, ALL_TIES, (topk - above).astype(F32)), (tq, LANES))
    any_ties = jnp.max((exact == 0).astype(I32)) > 0

    m_scr[...] = jnp.full(m_scr.shape, NEG_BIG, F32)
    l_scr[...] = jnp.zeros(l_scr.shape, F32)
    acc_scr[...] = jnp.zeros(acc_scr.shape, F32)
    ratio = tka // tk

    ntiles_a = (ntiles + ratio - 1) // ratio

    @pl.when(ntiles % ratio != 0)
    def _():
        sc_scr[ntiles] = jnp.full((tq, tk), -jnp.inf, F32)

    sub_tiles = [(r, j) for r in range(ratio) for j in range(nsub)]

    def mask_plain(ta, carry):
        for r, j in sub_tiles:
            js = slice(j * LANES, (j + 1) * LANES)
            sc_scr[ratio * ta + r, :, js] = jnp.where(sc_scr[ratio * ta + r, :, js] >= thr_b, 0.0, NEG_BIG)
        return carry

    def mask_with_ties(ta, seen):
        xs = [sc_scr[ratio * ta + r, :, j * LANES:(j + 1) * LANES] for r, j in sub_tiles]
        ties = [jnp.where(x == thr_b, 1.0, 0.0) for x in xs]
        counts = [jnp.dot(tie.astype(BF16), tri_ref[...], preferred_element_type=F32) for tie in ties]
        for i, (r, j) in enumerate(sub_tiles):
            prefix, total = counts[i][:, 0:LANES], counts[i][:, LANES:2 * LANES]
            keep = jnp.where(seen + prefix <= ties_b, ties[i], 0.0)
            sc_scr[ratio * ta + r, :, j * LANES:(j + 1) * LANES] = jnp.where(
                xs[i] > thr_b, 0.0, jnp.where(keep > 0.5, 0.0, NEG_BIG))
            seen = seen + total
        return seen

    def run_mask_with_ties():
        lax.fori_loop(0, ntiles_a, mask_with_ties, jnp.zeros((tq, LANES), F32))
        return 0

    def run_mask_plain():
        lax.fori_loop(0, ntiles_a, mask_plain, 0)
        return 0

    lax.cond(any_ties, run_mask_with_ties, run_mask_plain)

    def attn_tile(t, carry):
        s = jnp.dot(q_ref[0, 0], kt_ref[0, t], preferred_element_type=F32)
        for h in range(DSA_HEADS):
            r0 = h * tq
            sb = [s[r0:r0 + tq, j * LANES:(j + 1) * LANES]
                  + sc_scr[ratio * t + j // nsub, :, (j % nsub) * LANES:(j % nsub + 1) * LANES]
                  for j in range(nsuba)]
            m_old = m_scr[h]
            m_new = jnp.maximum(m_old, jnp.max(functools.reduce(jnp.maximum, sb), axis=1, keepdims=True))
            a = jnp.exp2(m_old - m_new)
            ps = [jnp.exp2(x - m_new) for x in sb]
            l_scr[h] = a * l_scr[h] + functools.reduce(jnp.add, ps)
            m_scr[h] = m_new
            a_scr[h] = a
            for j in range(nsuba):
                p_scr[r0:r0 + tq, j * LANES:(j + 1) * LANES] = ps[j].astype(BF16)
        pv = jnp.dot(p_scr[...], v_ref[0, t], preferred_element_type=F32)
        for h in range(DSA_HEADS):
            acc_scr[h] = a_scr[h][:, 0:HEAD_DIM] * acc_scr[h] + pv[h * tq:(h + 1) * tq, :]
        return carry

    def attn_tile_stale_ref(t, jump):
        s = jnp.dot(q_ref[0, 0], kt_ref[0, t], preferred_element_type=F32)
        for h in range(DSA_HEADS):
            r0 = h * tq
            m_ref = m_scr[h]
            sb = [s[r0:r0 + tq, j * LANES:(j + 1) * LANES]
                  + sc_scr[ratio * t + j // nsub, :, (j % nsub) * LANES:(j % nsub + 1) * LANES]
                  for j in range(nsuba)]
            ps = [jnp.exp2(x - m_ref) for x in sb]
            m_tile = jnp.max(functools.reduce(jnp.maximum, sb), axis=1, keepdims=True)
            m_new = jnp.maximum(m_ref, m_tile)
            a = jnp.exp2(m_ref - m_new)
            l_scr[h] = (l_scr[h] + functools.reduce(jnp.add, ps)) * a
            m_scr[h] = m_new
            a_scr[h] = a
            jump = jnp.maximum(jump, m_tile - m_ref)
            for j in range(nsuba):
                p_scr[r0:r0 + tq, j * LANES:(j + 1) * LANES] = ps[j].astype(BF16)
        pv = jnp.dot(p_scr[...], v_ref[0, t], preferred_element_type=F32)
        for h in range(DSA_HEADS):
            acc_scr[h] = (acc_scr[h] + pv[h * tq:(h + 1) * tq, :]) * a_scr[h][:, 0:HEAD_DIM]
        return jump

    def init_softmax_state():
        m_scr[...] = jnp.full(m_scr.shape, NEG_BIG, F32)
        l_scr[...] = jnp.zeros(l_scr.shape, F32)
        acc_scr[...] = jnp.zeros(acc_scr.shape, F32)

    ntiles_a = (ntiles + ratio - 1) // ratio
    attn_tile(0, 0)
    jump = _unrolled_tiles(1, ntiles_a, ATTN_UNROLL, attn_tile_stale_ref, jnp.zeros((tq, LANES), F32))
    safe = jnp.max(jump) <= MAX_STALE_JUMP

    @pl.when(jnp.logical_not(safe))
    def _():
        init_softmax_state()
        lax.fori_loop(0, ntiles_a, attn_tile, 0)

    o_ref[0] = jnp.concatenate([acc_scr[h] / jnp.sum(l_scr[h], axis=1, keepdims=True) for h in range(DSA_HEADS)],
                               axis=1).astype(BF16)


def dsa_attention(qst, kt, v, iq3, ik3, ikw):
    B, nqb = qst.shape[:2]
    tq, tk, tka = TQ_DSA, TK_DSA, TKA_DSA
    S = nqb * tq
    nkt, nka = S // tk, S // tka
    topk = min(TOPK_MAX, S // 4)
    per_batch = pl.Buffered(1)
    ones = np.ones((LANES, LANES), np.float32)
    tri = jnp.asarray(np.concatenate([np.triu(ones), ones], axis=1), BF16)
    return pl.pallas_call(
        functools.partial(_dsa_kernel, topk=topk),
        out_shape=jax.ShapeDtypeStruct((B, S, ATT_WIDTH), BF16),
        grid=(B, nqb),
        in_specs=[
            pl.BlockSpec((1, 1, IDX_HEADS * tq, LANES), lambda b, i: (b, i, 0, 0)),
            pl.BlockSpec((1, tq, LANES), lambda b, i: (b, i, 0)),
            pl.BlockSpec((1, nkt, LANES, tk), lambda b, i: (b, 0, 0, 0), pipeline_mode=per_batch),
            pl.BlockSpec((1, 1, DSA_HEADS * tq, HEAD_DIM), lambda b, i: (b, i, 0, 0)),
            pl.BlockSpec((1, nka, HEAD_DIM, tka), lambda b, i: (b, 0, 0, 0), pipeline_mode=per_batch),
            pl.BlockSpec((1, nka, tka, HEAD_DIM), lambda b, i: (b, 0, 0, 0), pipeline_mode=per_batch),
            pl.BlockSpec((LANES, 2 * LANES), lambda b, i: (0, 0)),
        ],
        out_specs=pl.BlockSpec((1, tq, ATT_WIDTH), lambda b, i: (b, i, 0)),
        scratch_shapes=[
            pltpu.VMEM((nkt, tq, tk), F32),
            pltpu.VMEM((IDX_HEADS, tq, LANES), F32),
            pltpu.VMEM((2, tq, LANES), F32),
            pltpu.VMEM((DSA_HEADS * tq, tka), BF16),
            pltpu.VMEM((DSA_HEADS, tq, LANES), F32),
            pltpu.VMEM((DSA_HEADS, tq, LANES), F32),
            pltpu.VMEM((DSA_HEADS, tq, LANES), F32),
            pltpu.VMEM((DSA_HEADS, tq, HEAD_DIM), F32),
        ],
        compiler_params=_cparams(("parallel", "arbitrary")),
    )(iq3, ikw, ik3, qst, kt, v, tri)


def _band_kernel(q_ref, k0_ref, k1_ref, k2_ref, v0_ref, v1_ref, v2_ref, bias_ref, o_ref):
    tq = TQ_BAND
    i = pl.program_id(1)
    q = (q_ref[0] * (HEAD_DIM ** -0.5)).astype(BF16)
    ks = [r[0].astype(BF16) for r in (k0_ref, k1_ref, k2_ref)]
    vs = [r[0].astype(BF16) for r in (v0_ref, v1_ref, v2_ref)]
    nt = len(ks)
    for h in range(CHK_HEADS):
        hs = slice(h * HEAD_DIM, (h + 1) * HEAD_DIM)
        parts = []
        for j in range(nt):
            s = lax.dot_general(q[:, hs], ks[j][:, hs], (((1,), (1,)), ((), ())), preferred_element_type=F32)
            s = s + bias_ref[h, :, j * tq:(j + 1) * tq]
            if j < nt - 1:
                s = jnp.where(i + j >= nt - 1, s, NEG_BIG)
            parts.append(s)
        m = functools.reduce(jnp.maximum, [jnp.max(s, axis=-1, keepdims=True) for s in parts])
        ps = [jnp.exp(s - m) for s in parts]
        l = functools.reduce(jnp.add, [jnp.sum(p, axis=-1, keepdims=True) for p in ps])
        o = functools.reduce(jnp.add, [jnp.dot(p.astype(BF16), vs[j][:, hs], preferred_element_type=F32)
                                        for j, p in enumerate(ps)])
        o_ref[0, :, hs] = (o / l).astype(BF16)


def _band_bias(rel_bias):
    tq = TQ_BAND
    nt = BAND_CHUNKS * CHUNK // tq + 1
    win = nt * tq
    band = (BAND_CHUNKS + 1) * CHUNK
    H = rel_bias.shape[0]
    n = band + CHUNK - 1
    g = jnp.concatenate([rel_bias, jnp.broadcast_to(rel_bias[:, -1:], (H, n - rel_bias.shape[1]))], axis=1)[:, ::-1]
    hankel = jnp.tile(g, (1, CHUNK + 1))[:, :CHUNK * (n + 1)].reshape(H, CHUNK, n + 1)[:, :, :band]
    w = hankel[:, ::-1, :]
    rows = [jnp.pad(w, ((0, 0), (0, 0), (a * CHUNK, win - band - a * CHUNK)), constant_values=NEG_BIG)
            for a in range(tq // CHUNK)]
    return jnp.concatenate(rows, axis=1).astype(F32)


def band_attention(proj, rel_bias):
    B, S, _ = proj.shape
    tq = TQ_BAND
    nt = BAND_CHUNKS * CHUNK // tq + 1
    bias = _band_bias(rel_bias)

    def kspec(col, j):
        return pl.BlockSpec((1, tq, ATT_WIDTH), lambda b, i: (b, jnp.maximum(i + j - (nt - 1), 0), col // ATT_WIDTH))

    return pl.pallas_call(
        _band_kernel,
        out_shape=jax.ShapeDtypeStruct((B, S, ATT_WIDTH), BF16),
        grid=(B, S // tq),
        in_specs=[pl.BlockSpec((1, tq, ATT_WIDTH), lambda b, i: (b, i, COL_CQ // ATT_WIDTH))]
                 + [kspec(COL_CK, j) for j in range(nt)] + [kspec(COL_CV, j) for j in range(nt)]
                 + [pl.BlockSpec((CHK_HEADS, tq, nt * tq), lambda b, i: (0, 0, 0))],
        out_specs=pl.BlockSpec((1, tq, ATT_WIDTH), lambda b, i: (b, i, 0)),
        compiler_params=_cparams(("parallel", "parallel")),
    )(proj, proj, proj, proj, proj, proj, proj, bias)


def _outproj_kernel(yc_ref, yd_ref, yb_ref, x_ref, w_ref, g_ref, b_ref, o_ref):
    c0, c1 = CONV_WIDTH, CONV_WIDTH + ATT_WIDTH
    mix = jnp.dot(yc_ref[...], w_ref[0:c0, :], preferred_element_type=F32)
    mix = mix + jnp.dot(yd_ref[...], w_ref[c0:c1, :], preferred_element_type=F32)
    mix = mix + jnp.dot(yb_ref[...], w_ref[c1:D_MODEL, :], preferred_element_type=F32)
    o_ref[...] = _layer_norm(ALPHA * x_ref[...] + mix, g_ref[...], b_ref[...])


def outproj_ln(yc, yd, yb, x, w, g, b):
    T = x.shape[0]
    tm = min(TM_OUT, T)
    rows = lambda width: pl.BlockSpec((tm, width), lambda i: (i, 0))
    vec = pl.BlockSpec((1, D_MODEL), lambda i: (0, 0))
    return pl.pallas_call(
        _outproj_kernel,
        out_shape=jax.ShapeDtypeStruct((T, D_MODEL), F32),
        grid=(T // tm,),
        in_specs=[rows(CONV_WIDTH), rows(ATT_WIDTH), rows(ATT_WIDTH), rows(D_MODEL),
                  pl.BlockSpec((D_MODEL, D_MODEL), lambda i: (0, 0)), vec, vec],
        out_specs=rows(D_MODEL),
        compiler_params=_cparams(("parallel",)),
    )(yc, yd, yb, x, w, g, b)


def _cross_kernel(x_ref, wq_ref, k_ref, v_ref, wo_ref, g_ref, b_ref, o_ref):
    x = x_ref[0]
    q = jnp.dot(x.astype(BF16), wq_ref[...], preferred_element_type=F32)
    q = (q * (MEM_HEAD_DIM ** -0.5)).astype(BF16)
    k = k_ref[0]
    v = v_ref[0]
    outs = []
    for h in range(MEM_HEADS):
        hs = slice(h * MEM_HEAD_DIM, (h + 1) * MEM_HEAD_DIM)
        s = lax.dot_general(q[:, hs], k[:, hs], (((1,), (1,)), ((), ())), preferred_element_type=F32)
        p = jnp.exp(s - jnp.max(s, axis=-1, keepdims=True))
        p = p / jnp.sum(p, axis=-1, keepdims=True)
        outs.append(jnp.dot(p.astype(BF16), v[:, hs], preferred_element_type=F32).astype(BF16))
    o = jnp.concatenate(outs, axis=-1)
    cross = jnp.dot(o, wo_ref[...], preferred_element_type=F32)
    o_ref[0] = _layer_norm(ALPHA * x + cross, g_ref[...], b_ref[...])


def cross_ln(x, k, v, wq, wo, g, b):
    B, S, _ = x.shape
    M = k.shape[1]
    tm = min(TM_CROSS, S)
    full = lambda shape: pl.BlockSpec(shape, lambda bb, i: (0,) * len(shape))
    return pl.pallas_call(
        _cross_kernel,
        out_shape=jax.ShapeDtypeStruct((B, S, D_MODEL), F32),
        grid=(B, S // tm),
        in_specs=[pl.BlockSpec((1, tm, D_MODEL), lambda bb, i: (bb, i, 0)),
                  full((D_MODEL, D_MODEL)),
                  pl.BlockSpec((1, M, D_MODEL), lambda bb, i: (bb, 0, 0)),
                  pl.BlockSpec((1, M, D_MODEL), lambda bb, i: (bb, 0, 0)),
                  full((D_MODEL, D_MODEL)), full((1, D_MODEL)), full((1, D_MODEL))],
        out_specs=pl.BlockSpec((1, tm, D_MODEL), lambda bb, i: (bb, i, 0)),
        compiler_params=_cparams(("parallel", "parallel")),
    )(x, wq, k, v, wo, g, b)


def _mlp_kernel(x_ref, w1_ref, w2_ref, g_ref, b_ref, o_ref, acc_ref):
    j = pl.program_id(1)

    @pl.when(j == 0)
    def _():
        acc_ref[...] = jnp.zeros(acc_ref.shape, F32)

    h = jnp.dot(x_ref[...].astype(BF16), w1_ref[...], preferred_element_type=F32)
    h = jnp.square(jnp.maximum(h, 0.0)).astype(BF16)
    acc_ref[...] += jnp.dot(h, w2_ref[...], preferred_element_type=F32)

    @pl.when(j == pl.num_programs(1) - 1)
    def _():
        o_ref[...] = _layer_norm(ALPHA * x_ref[...] + acc_ref[...], g_ref[...], b_ref[...])


def mlp_ln(x, w1, w2, g, b):
    T = x.shape[0]
    tm = min(TM_MLP, T)
    tf = TF_MLP
    vec = pl.BlockSpec((1, D_MODEL), lambda i, j: (0, 0))
    return pl.pallas_call(
        _mlp_kernel,
        out_shape=jax.ShapeDtypeStruct((T, D_MODEL), F32),
        grid=(T // tm, D_FF // tf),
        in_specs=[pl.BlockSpec((tm, D_MODEL), lambda i, j: (i, 0)),
                  pl.BlockSpec((D_MODEL, tf), lambda i, j: (0, j)),
                  pl.BlockSpec((tf, D_MODEL), lambda i, j: (j, 0)), vec, vec],
        out_specs=pl.BlockSpec((tm, D_MODEL), lambda i, j: (i, 0)),
        scratch_shapes=[pltpu.VMEM((tm, D_MODEL), F32)],
        compiler_params=_cparams(("parallel", "arbitrary")),
    )(x, w1, w2, g, b)


def _permute_w_in(w_in):
    sizes = (CONV_WIDTH, CONV_WIDTH, CONV_WIDTH, ATT_WIDTH, HEAD_DIM, HEAD_DIM,
             IDX_HEADS * IDX_DIM, IDX_DIM, IDX_HEADS, ATT_WIDTH, ATT_WIDTH, ATT_WIDTH)
    offs = np.concatenate([[0], np.cumsum(sizes)])
    seg = [w_in[:, offs[k]:offs[k + 1]] for k in range(len(sizes))]
    cb, cc, ch, dq, dk, dv, iq, ik, iw, cq, ck, cv = seg
    pad = jnp.zeros((w_in.shape[0], LANES - IDX_DIM - IDX_HEADS), w_in.dtype)
    return jnp.concatenate([cq, ck, cv, dq, cb, cc, ch, iq, dk, dv, ik, iw, pad], axis=1).astype(BF16)


@jax.jit
def kernel(x, mem, positions, w_in, conv_w, rel_bias, w_mix_out, ln1_g, ln1_b,
           w_mq, w_mkv, w_mo, ln2_g, ln2_b, w_ff1, w_ff2, ln3_g, ln3_b):
    B, S, D = x.shape
    T = B * S
    M = mem.shape[1]
    tabs_q = rope_tables(positions, HEAD_DIM, ROPE_DIM)
    tabs_i = rope_tables(positions, IDX_DIM, IDX_ROPE_DIM)
    mem2 = mem.reshape(B * M, D)
    xf = x.reshape(T, D)
    vec = lambda a: a.reshape(1, D)
    for l in range(w_in.shape[0]):
        proj = matmul(xf, _permute_w_in(w_in[l]), min(TM_PROJ, T)).reshape(B, S, PROJ_W)
        yconv, qst, kt, v, iq3, ik3, ikw = prep(proj, tabs_q, tabs_i, conv_w[l])
        ydsa = dsa_attention(qst, kt, v, iq3, ik3, ikw)
        yband = band_attention(proj, rel_bias[l])
        xf = outproj_ln(yconv.reshape(T, -1), ydsa.reshape(T, -1), yband.reshape(T, -1), xf,
                        w_mix_out[l].astype(BF16), vec(ln1_g[l]), vec(ln1_b[l]))
        kvm = matmul(mem2, w_mkv[l].astype(BF16), min(512, B * M)).reshape(B, M, 2 * D)
        xf = cross_ln(xf.reshape(B, S, D), kvm[..., :D].astype(BF16), kvm[..., D:].astype(BF16),
                      w_mq[l].astype(BF16), w_mo[l].astype(BF16), vec(ln2_g[l]), vec(ln2_b[l])).reshape(T, D)
        xf = mlp_ln(xf, w_ff1[l].astype(BF16), w_ff2[l].astype(BF16), vec(ln3_g[l]), vec(ln3_b[l]))
    return xf.reshape(B, S, D)
```

```python
import functools

import numpy as np
import jax
import jax.numpy as jnp
from jax import lax
from jax.experimental import pallas as pl
from jax.experimental.pallas import tpu as pltpu

F32 = jnp.float32
BF16 = jnp.bfloat16
I32 = jnp.int32

D_MODEL = 1024
CHUNK = 64
HEAD_DIM = 64
CONV_WIDTH = 256
CONV_K = 3
ATT_WIDTH = 384
DSA_HEADS = 6
CHK_HEADS = 6
ROPE_DIM = 16
ROPE_THETA = 500000.0
IDX_HEADS = 8
IDX_DIM = 32
IDX_ROPE_DIM = 8
TOPK_MAX = 256
BAND_CHUNKS = 8
REL_CLIP = 128
MEM_HEADS = 4
MEM_HEAD_DIM = 256
D_FF = 4096
LN_EPS = 1e-5
DEPTH = 4
ALPHA = (2.0 * DEPTH) ** 0.25

LANES = 128
SUBLANES = 8
VMEM_LIMIT = 52 * 1024 * 1024

COL_CQ, COL_CK, COL_CV = 0, 384, 768
COL_DQ = 1152
COL_CONV = 1536
COL_IQ = 2304
COL_DKV = 2560
COL_IKW = 2688
PROJ_W = 2816

TM_PROJ = 512
TM_PREP = 512
TQ_DSA = 128
TK_DSA = 512
TKA_DSA = 1024
SCORE_UNROLL = 4
COUNT_UNROLL = 4
ATTN_UNROLL = 4
MASK_UNROLL = 2
TQ_BAND = 256
TM_OUT = 1024
TM_CROSS = 1024
TM_MLP = 1024
TF_MLP = 1024

LOG2_E = 1.4426950408889634
NEG_BIG = -1e30
ALL_TIES = 1e9
MAX_STALE_JUMP = 100.0
BISECT_UNCHECKED_STEPS = 14
BISECT_VALUE_STEPS = 12
SNAP_FIRST_STEP = 14
SNAP_EVERY = 6


def _cparams(sem):
    return pltpu.CompilerParams(dimension_semantics=sem, vmem_limit_bytes=VMEM_LIMIT)


def _layer_norm(y, g, b):
    mu = jnp.mean(y, axis=-1, keepdims=True)
    d = y - mu
    var = jnp.mean(d * d, axis=-1, keepdims=True)
    return d * lax.rsqrt(var + LN_EPS) * g + b


def _rope_table_kernel(pos_ref, freq_ref, mc_ref, m1_ref, m2_ref, c_ref, s1_ref, s2_ref):
    ang = pos_ref[0].astype(F32) * freq_ref[...]
    c = jnp.cos(ang)
    s = jnp.sin(ang)
    mc = mc_ref[...]
    c_ref[0] = c * mc + (1.0 - mc)
    s1_ref[0] = s * m1_ref[...]
    s2_ref[0] = -(s * m2_ref[...])


def _rope_lane_consts(head_dim, rot_dim):
    half = rot_dim // 2
    inv_freq = np.power(np.float32(ROPE_THETA), -np.arange(half, dtype=np.float32) / np.float32(half)).astype(np.float32)
    lane = np.arange(LANES)
    dd = lane % head_dim
    freq = np.where(dd < rot_dim, inv_freq[lane % half], 0.0).astype(np.float32)
    mc = (dd < rot_dim).astype(np.float32)
    m1 = ((dd >= half) & (dd < rot_dim)).astype(np.float32)
    m2 = (dd < half).astype(np.float32)
    return [jnp.asarray(a.reshape(1, LANES)) for a in (freq, mc, m1, m2)]


def rope_tables(positions, head_dim, rot_dim):
    B, S = positions.shape
    tm = min(1024, S)
    consts = _rope_lane_consts(head_dim, rot_dim)
    row = pl.BlockSpec((1, LANES), lambda b, i: (0, 0))
    tab = pl.BlockSpec((1, tm, LANES), lambda b, i: (b, i, 0))
    out = jax.ShapeDtypeStruct((B, S, LANES), F32)
    return pl.pallas_call(
        _rope_table_kernel,
        out_shape=(out, out, out),
        grid=(B, S // tm),
        in_specs=[pl.BlockSpec((1, tm, 1), lambda b, i: (b, i, 0)), row, row, row, row],
        out_specs=(tab, tab, tab),
        compiler_params=_cparams(("parallel", "parallel")),
    )(positions.reshape(B, S, 1), *consts)


def _mm_kernel(x_ref, w_ref, o_ref):
    o_ref[...] = jnp.dot(x_ref[...].astype(BF16), w_ref[...], preferred_element_type=F32)


def matmul(x, w, tm):
    M, K = x.shape
    N = w.shape[1]
    return pl.pallas_call(
        _mm_kernel,
        out_shape=jax.ShapeDtypeStruct((M, N), F32),
        grid=(M // tm,),
        in_specs=[pl.BlockSpec((tm, K), lambda i: (i, 0)), pl.BlockSpec((K, N), lambda i: (0, 0))],
        out_specs=pl.BlockSpec((tm, N), lambda i: (i, 0)),
        compiler_params=_cparams(("parallel",)),
    )(x, w)


def _rope_apply(x, c, s1, s2, half):
    return x * c + pltpu.roll(x, half, 1) * s1 + pltpu.roll(x, LANES - half, 1) * s2


def _prep_kernel(dq_ref, conv_ref, halo_ref, iq_ref, dkv_ref, ikw_ref,
                 cq_ref, s1q_ref, s2q_ref, ci_ref, s1i_ref, s2i_ref, cw_ref,
                 yconv_ref, q_ref, kt_ref, v_ref, iq3_ref, ik3_ref, ikwr_ref, u_scr):
    tm = dq_ref.shape[1]
    i = pl.program_id(1)
    cv = conv_ref[0]
    gate_b = cv[:, 0:CONV_WIDTH]
    u = cv[:, CONV_WIDTH:2 * CONV_WIDTH] * cv[:, 2 * CONV_WIDTH:3 * CONV_WIDTH]
    hv = halo_ref[0]
    hu = hv[:, CONV_WIDTH:2 * CONV_WIDTH] * hv[:, 2 * CONV_WIDTH:3 * CONV_WIDTH]
    hu = jnp.where(i > 0, hu, 0.0)
    u_scr[0:SUBLANES, :] = hu
    u_scr[SUBLANES:SUBLANES + tm, :] = u
    u1 = u_scr[SUBLANES - 1:SUBLANES - 1 + tm, :]
    u2 = u_scr[SUBLANES - 2:SUBLANES - 2 + tm, :]
    cw = cw_ref[...]
    y = cw[0:1, :] * u2 + cw[1:2, :] * u1 + cw[2:3, :] * u
    yconv_ref[0] = (gate_b * y).astype(BF16)
    cq, s1q, s2q = cq_ref[0], s1q_ref[0], s2q_ref[0]
    ci, s1i, s2i = ci_ref[0], s1i_ref[0], s2i_ref[0]
    hq = ROPE_DIM // 2
    hi = IDX_ROPE_DIM // 2
    dq = dq_ref[0]
    scale = HEAD_DIM ** -0.5 * LOG2_E
    tq = TQ_DSA
    qr = [_rope_apply(dq[:, j * LANES:(j + 1) * LANES], cq, s1q, s2q, hq) * scale for j in range(ATT_WIDTH // LANES)]
    per_tile = LANES // HEAD_DIM
    for blk in range(tm // tq):
        rows = slice(blk * tq, (blk + 1) * tq)
        for h in range(DSA_HEADS):
            c0 = (h % per_tile) * HEAD_DIM
            q_ref[0, blk, h * tq:(h + 1) * tq, :] = qr[h // per_tile][rows, c0:c0 + HEAD_DIM].astype(BF16)
    lane = lax.broadcasted_iota(I32, (1, LANES), 1)
    is_k = lane < HEAD_DIM
    kvr = _rope_apply(dkv_ref[0], jnp.where(is_k, cq, 1.0), jnp.where(is_k, s1q, 0.0),
                      jnp.where(is_k, s2q, 0.0), hq)
    kt_ref[0, 0] = kvr.T[0:HEAD_DIM, :].astype(BF16)
    v_ref[0, 0] = kvr[:, HEAD_DIM:2 * HEAD_DIM].astype(BF16)

    def hi_lo(x):
        x_hi = x.astype(BF16).astype(F32)
        return x_hi, x - x_hi

    iq = iq_ref[0]
    iqr = [_rope_apply(iq[:, j * LANES:(j + 1) * LANES], ci, s1i, s2i, hi) for j in range(IDX_HEADS * IDX_DIM // LANES)]
    per_tile = LANES // IDX_DIM
    zero_q = jnp.zeros((tq, IDX_DIM), F32)
    for blk in range(tm // tq):
        rows = slice(blk * tq, (blk + 1) * tq)
        for h in range(IDX_HEADS):
            c0 = (h % per_tile) * IDX_DIM
            x_hi, x_lo = hi_lo(iqr[h // per_tile][rows, c0:c0 + IDX_DIM])
            iq3_ref[0, blk, h * tq:(h + 1) * tq, :] = jnp.concatenate([x_hi, x_lo, x_hi, zero_q], axis=1).astype(BF16)
    is_ik = lane < IDX_DIM
    is_iw = (lane >= IDX_DIM) & (lane < IDX_DIM + IDX_HEADS)
    ikw = _rope_apply(ikw_ref[0], jnp.where(is_ik, ci, 1.0), jnp.where(is_ik, s1i, 0.0),
                      jnp.where(is_ik, s2i, 0.0), hi)
    ikwr_ref[0] = ikw * jnp.where(is_iw, IDX_HEADS ** -0.5, 1.0)
    k_hi, k_lo = hi_lo(ikw[:, 0:IDX_DIM])
    ik3 = jnp.concatenate([k_hi, k_hi, k_lo, jnp.zeros((tm, IDX_DIM), F32)], axis=1)
    ik3_ref[0, 0] = ik3.T.astype(BF16)


def prep(proj, tabs_q, tabs_i, conv_w):
    B, S, _ = proj.shape
    tm = min(TM_PREP, S)
    hb = tm // SUBLANES

    def col(width, start):
        return pl.BlockSpec((1, tm, width), lambda b, i: (b, i, start // width))

    tab = pl.BlockSpec((1, tm, LANES), lambda b, i: (b, i, 0))
    halo = pl.BlockSpec((1, SUBLANES, 3 * CONV_WIDTH),
                        lambda b, i: (b, jnp.maximum(i * hb - 1, 0), COL_CONV // (3 * CONV_WIDTH)))

    def out(width, dtype):
        return jax.ShapeDtypeStruct((B, S, width), dtype), pl.BlockSpec((1, tm, width), lambda b, i: (b, i, 0))

    tq, tk, tka = TQ_DSA, TK_DSA, TKA_DSA
    assert tm == tk and tka % tm == 0 and S % tka == 0
    nqb, bq, r = S // tq, tm // tq, tka // tm
    outs = [out(CONV_WIDTH, BF16),
            (jax.ShapeDtypeStruct((B, nqb, DSA_HEADS * tq, HEAD_DIM), BF16),
             pl.BlockSpec((1, bq, DSA_HEADS * tq, HEAD_DIM), lambda b, i: (b, i, 0, 0))),
            (jax.ShapeDtypeStruct((B, S // tka, HEAD_DIM, tka), BF16),
             pl.BlockSpec((1, 1, HEAD_DIM, tm), lambda b, i: (b, i // r, 0, i % r))),
            (jax.ShapeDtypeStruct((B, S // tka, tka, HEAD_DIM), BF16),
             pl.BlockSpec((1, 1, tm, HEAD_DIM), lambda b, i: (b, i // r, i % r, 0))),
            (jax.ShapeDtypeStruct((B, nqb, IDX_HEADS * tq, LANES), BF16),
             pl.BlockSpec((1, bq, IDX_HEADS * tq, LANES), lambda b, i: (b, i, 0, 0))),
            (jax.ShapeDtypeStruct((B, S // tk, LANES, tk), BF16),
             pl.BlockSpec((1, 1, LANES, tk), lambda b, i: (b, i, 0, 0))),
            out(LANES, F32)]
    return pl.pallas_call(
        _prep_kernel,
        out_shape=tuple(o[0] for o in outs),
        grid=(B, S // tm),
        in_specs=[col(ATT_WIDTH, COL_DQ), col(3 * CONV_WIDTH, COL_CONV), halo,
                  col(IDX_HEADS * IDX_DIM, COL_IQ), col(LANES, COL_DKV), col(LANES, COL_IKW),
                  tab, tab, tab, tab, tab, tab,
                  pl.BlockSpec((CONV_K, CONV_WIDTH), lambda b, i: (0, 0))],
        out_specs=tuple(o[1] for o in outs),
        scratch_shapes=[pltpu.VMEM((tm + SUBLANES, CONV_WIDTH), F32)],
        compiler_params=_cparams(("parallel", "parallel")),
    )(proj, proj, proj, proj, proj, proj, *tabs_q, *tabs_i, conv_w)


def _sortable_key_np(v):
    b = np.array([v], dtype=np.float32).view(np.int32)[0]
    return int(b ^ ((b >> 31) & 0x7FFFFFFF))


KEY_LOWEST_FINITE = _sortable_key_np(-3.0e38)
KEY_NEG_INF = _sortable_key_np(-np.inf)
KEY_POS_INF = _sortable_key_np(np.inf)


def _key_of_float(x):
    bits = pltpu.bitcast(x, I32)
    return bits ^ ((bits >> 31) & 0x7FFFFFFF)


def _float_of_key(k):
    return pltpu.bitcast(k ^ ((k >> 31) & 0x7FFFFFFF), F32)


def _unrolled_tiles(start, stop, unroll, step, carry):
    ngroups = (stop - start) // unroll

    def group(g, c):
        for u in range(unroll):
            c = step(start + g * unroll + u, c)
        return c

    carry = lax.fori_loop(0, ngroups, group, carry)
    for u in range(unroll - 1):
        t = start + ngroups * unroll + u
        carry = lax.cond(t < stop, lambda c: step(t, c), lambda c: c, carry)
    return carry


def _dsa_kernel(iq3_ref, iw_ref, ik3_ref, q_ref, kt_ref, v_ref, tri_ref, o_ref,
                sc_scr, iwb_scr, cmax_scr, p_scr,
                m_scr, l_scr, a_scr, acc_scr, *, topk):
    tq = TQ_DSA
    tk = TK_DSA
    tka = TKA_DSA
    nsub = tk // LANES
    nsuba = tka // LANES
    qb = pl.program_id(1)
    nk = (qb + 1) * tq
    ntiles = (nk + tk - 1) // tk
    iw = iw_ref[0]
    for h in range(IDX_HEADS):
        iwb_scr[h] = jnp.broadcast_to(iw[:, IDX_DIM + h:IDX_DIM + h + 1], (tq, LANES))
    cmax_scr[...] = jnp.full(cmax_scr.shape, -jnp.inf, F32)

    def score_tile(t, last):
        if last:
            row = lax.broadcasted_iota(I32, (tq, 1), 0)
            lim = ((qb * tq + row) // CHUNK + 1) * CHUNK
            lane = lax.broadcasted_iota(I32, (tq, LANES), 1)
        lg = jnp.dot(iq3_ref[0, 0], ik3_ref[0, t], preferred_element_type=F32)
        for j in range(nsub):
            js = slice(j * LANES, (j + 1) * LANES)
            sc = iwb_scr[0] * jnp.maximum(lg[0:tq, js], 0.0)
            for h in range(1, IDX_HEADS):
                sc = sc + iwb_scr[h] * jnp.maximum(lg[h * tq:(h + 1) * tq, js], 0.0)
            if last:
                sc = jnp.where(t * tk + j * LANES + lane < lim, sc, -jnp.inf)
            cmax_scr[j % 2] = jnp.maximum(cmax_scr[j % 2], sc)
            sc_scr[t, :, js] = sc

    def score_body(t, carry):
        score_tile(t, False)
        return carry

    _unrolled_tiles(0, ntiles - 1, SCORE_UNROLL, score_body, 0)
    score_tile(ntiles - 1, True)

    def count_ge(mid):
        vf = jnp.broadcast_to(_float_of_key(mid), (tq, LANES))

        def count_tile(t, cnt):
            for j in range(nsub):
                cnt = cnt + (sc_scr[t, :, j * LANES:(j + 1) * LANES] >= vf).astype(I32)
            return cnt

        cnt = _unrolled_tiles(0, ntiles, COUNT_UNROLL, count_tile, jnp.zeros((tq, LANES), I32))
        return jnp.sum(cnt, axis=1, keepdims=True)

    def bisect_step(c):
        it, lo, hi, above, exact = c
        int_mid = (lo | hi) - ((lo ^ hi) >> 1)
        val_mid = _key_of_float(0.5 * _float_of_key(lo) + 0.5 * _float_of_key(hi))
        val_mid = jnp.minimum(hi, jnp.maximum(val_mid, lo + 1))
        mid = jnp.where(it < BISECT_VALUE_STEPS, val_mid, int_mid)
        total = count_ge(mid)
        ge = total >= topk
        eq = total == topk
        return (it + 1, jnp.where(ge, mid, lo), jnp.where(eq, mid, jnp.where(ge, hi, mid - 1)),
                jnp.where(ge, above, total), exact | eq.astype(I32))

    def snap_step(c):
        it, lo, hi, above, exact = c
        lo_b = jnp.broadcast_to(_float_of_key(lo), (tq, LANES))
        hi_b = jnp.broadcast_to(_float_of_key(hi), (tq, LANES))

        def range_tile(t, mm):
            mn, mx = mm
            for j in range(nsub):
                x = sc_scr[t, :, j * LANES:(j + 1) * LANES]
                inside = (x >= lo_b) & (x <= hi_b)
                mn = jnp.minimum(mn, jnp.where(inside, x, jnp.inf))
                mx = jnp.maximum(mx, jnp.where(inside, x, -jnp.inf))
            return mn, mx

        mn, mx = lax.fori_loop(0, ntiles, range_tile,
                               (jnp.full((tq, LANES), jnp.inf, F32), jnp.full((tq, LANES), -jnp.inf, F32)))
        open_ = hi > lo
        lo_n = jnp.where(open_, _key_of_float(jnp.min(mn, axis=1, keepdims=True)), lo)
        hi_n = jnp.where(open_, _key_of_float(jnp.max(mx, axis=1, keepdims=True)), hi)
        return it + 1, lo_n, hi_n, above, exact

    def search_cond(c):
        return jnp.max((c[2] > c[1]).astype(I32)) > 0

    def search_body(c):
        it = c[0]
        is_snap = (it >= SNAP_FIRST_STEP) & ((it - SNAP_FIRST_STEP) % SNAP_EVERY == 0)
        return lax.cond(is_snap, snap_step, bisect_step, c)

    cm = cmax_scr[...]
    lo_f = jnp.min(jnp.minimum(cm[0], cm[1]), axis=1, keepdims=True)
    hi_f = jnp.max(jnp.maximum(cm[0], cm[1]), axis=1, keepdims=True)
    zeros_i = jnp.zeros((tq, 1), I32)
    state = (jnp.int32(0), _key_of_float(lo_f), _key_of_float(hi_f), zeros_i, zeros_i)
    state = lax.fori_loop(0, BISECT_UNCHECKED_STEPS, lambda i, c: bisect_step(c), state)
    _, lo, _, above, exact = lax.while_loop(search_cond, search_body, state)
    thr = _float_of_key(jnp.maximum(lo, KEY_LOWEST_FINITE))
    thr_b = jnp.broadcast_to(thr, (tq, LANES))
    ties_b = jnp.broadcast_to(jnp.where(exact != 0, ALL_TIES, (topk - above).astype(F32)), (tq, LANES))
    any_ties = jnp.max((exact == 0).astype(I32)) > 0

    m_scr[...] = jnp.full(m_scr.shape, NEG_BIG, F32)
    l_scr[...] = jnp.zeros(l_scr.shape, F32)
    acc_scr[...] = jnp.zeros(acc_scr.shape, F32)
    ratio = tka // tk

    ntiles_a = (ntiles + ratio - 1) // ratio

    @pl.when(ntiles % ratio != 0)
    def _():
        sc_scr[ntiles] = jnp.full((tq, tk), -jnp.inf, F32)

    sub_tiles = [(r, j) for r in range(ratio) for j in range(nsub)]

    def mask_plain(ta, carry):
        for r, j in sub_tiles:
            js = slice(j * LANES, (j + 1) * LANES)
            sc_scr[ratio * ta + r, :, js] = jnp.where(sc_scr[ratio * ta + r, :, js] >= thr_b, 0.0, NEG_BIG)
        return carry

    def mask_with_ties(ta, seen):
        xs = [sc_scr[ratio * ta + r, :, j * LANES:(j + 1) * LANES] for r, j in sub_tiles]
        ties = [jnp.where(x == thr_b, 1.0, 0.0) for x in xs]
        counts = [jnp.dot(tie.astype(BF16), tri_ref[...], preferred_element_type=F32) for tie in ties]
        for i, (r, j) in enumerate(sub_tiles):
            prefix, total = counts[i][:, 0:LANES], counts[i][:, LANES:2 * LANES]
            keep = jnp.where(seen + prefix <= ties_b, ties[i], 0.0)
            sc_scr[ratio * ta + r, :, j * LANES:(j + 1) * LANES] = jnp.where(
                xs[i] > thr_b, 0.0, jnp.where(keep > 0.5, 0.0, NEG_BIG))
            seen = seen + total
        return seen

    def run_mask_with_ties():
        _unrolled_tiles(0, ntiles_a, MASK_UNROLL, mask_with_ties, jnp.zeros((tq, LANES), F32))
        return 0

    def run_mask_plain():
        _unrolled_tiles(0, ntiles_a, MASK_UNROLL, mask_plain, 0)
        return 0

    lax.cond(any_ties, run_mask_with_ties, run_mask_plain)

    def attn_tile(t, carry):
        s = jnp.dot(q_ref[0, 0], kt_ref[0, t], preferred_element_type=F32)
        for h in range(DSA_HEADS):
            r0 = h * tq
            sb = [s[r0:r0 + tq, j * LANES:(j + 1) * LANES]
                  + sc_scr[ratio * t + j // nsub, :, (j % nsub) * LANES:(j % nsub + 1) * LANES]
                  for j in range(nsuba)]
            m_old = m_scr[h]
            m_new = jnp.maximum(m_old, jnp.max(functools.reduce(jnp.maximum, sb), axis=1, keepdims=True))
            a = jnp.exp2(m_old - m_new)
            ps = [jnp.exp2(x - m_new) for x in sb]
            l_scr[h] = a * l_scr[h] + functools.reduce(jnp.add, ps)
            m_scr[h] = m_new
            a_scr[h] = a
            for j in range(nsuba):
                p_scr[r0:r0 + tq, j * LANES:(j + 1) * LANES] = ps[j].astype(BF16)
        pv = jnp.dot(p_scr[...], v_ref[0, t], preferred_element_type=F32)
        for h in range(DSA_HEADS):
            acc_scr[h] = a_scr[h][:, 0:HEAD_DIM] * acc_scr[h] + pv[h * tq:(h + 1) * tq, :]
        return carry

    def attn_tile_stale_ref(t, jump):
        s = jnp.dot(q_ref[0, 0], kt_ref[0, t], preferred_element_type=F32)
        for h in range(DSA_HEADS):
            r0 = h * tq
            m_ref = m_scr[h]
            sb = [s[r0:r0 + tq, j * LANES:(j + 1) * LANES]
                  + sc_scr[ratio * t + j // nsub, :, (j % nsub) * LANES:(j % nsub + 1) * LANES]
                  for j in range(nsuba)]
            ps = [jnp.exp2(x - m_ref) for x in sb]
            m_tile = jnp.max(functools.reduce(jnp.maximum, sb), axis=1, keepdims=True)
            m_new = jnp.maximum(m_ref, m_tile)
            a = jnp.exp2(m_ref - m_new)
            l_scr[h] = (l_scr[h] + functools.reduce(jnp.add, ps)) * a
            m_scr[h] = m_new
            a_scr[h] = a
            jump = jnp.maximum(jump, m_tile - m_ref)
            for j in range(nsuba):
                p_scr[r0:r0 + tq, j * LANES:(j + 1) * LANES] = ps[j].astype(BF16)
        pv = jnp.dot(p_scr[...], v_ref[0, t], preferred_element_type=F32)
        for h in range(DSA_HEADS):
            acc_scr[h] = (acc_scr[h] + pv[h * tq:(h + 1) * tq, :]) * a_scr[h][:, 0:HEAD_DIM]
        return jump

    def init_softmax_state():
        m_scr[...] = jnp.full(m_scr.shape, NEG_BIG, F32)
        l_scr[...] = jnp.zeros(l_scr.shape, F32)
        acc_scr[...] = jnp.zeros(acc_scr.shape, F32)

    ntiles_a = (ntiles + ratio - 1) // ratio
    attn_tile(0, 0)
    jump = _unrolled_tiles(1, ntiles_a, ATTN_UNROLL, attn_tile_stale_ref, jnp.zeros((tq, LANES), F32))
    safe = jnp.max(jump) <= MAX_STALE_JUMP

    @pl.when(jnp.logical_not(safe))
    def _():
        init_softmax_state()
        lax.fori_loop(0, ntiles_a, attn_tile, 0)

    o_ref[0] = jnp.concatenate([acc_scr[h] / jnp.sum(l_scr[h], axis=1, keepdims=True) for h in range(DSA_HEADS)],
                               axis=1).astype(BF16)


def dsa_attention(qst, kt, v, iq3, ik3, ikw):
    B, nqb = qst.shape[:2]
    tq, tk, tka = TQ_DSA, TK_DSA, TKA_DSA
    S = nqb * tq
    nkt, nka = S // tk, S // tka
    topk = min(TOPK_MAX, S // 4)
    per_batch = pl.Buffered(1)
    ones = np.ones((LANES, LANES), np.float32)
    tri = jnp.asarray(np.concatenate([np.triu(ones), ones], axis=1), BF16)
    return pl.pallas_call(
        functools.partial(_dsa_kernel, topk=topk),
        out_shape=jax.ShapeDtypeStruct((B, S, ATT_WIDTH), BF16),
        grid=(B, nqb),
        in_specs=[
            pl.BlockSpec((1, 1, IDX_HEADS * tq, LANES), lambda b, i: (b, i, 0, 0)),
            pl.BlockSpec((1, tq, LANES), lambda b, i: (b, i, 0)),
            pl.BlockSpec((1, nkt, LANES, tk), lambda b, i: (b, 0, 0, 0), pipeline_mode=per_batch),
            pl.BlockSpec((1, 1, DSA_HEADS * tq, HEAD_DIM), lambda b, i: (b, i, 0, 0)),
            pl.BlockSpec((1, nka, HEAD_DIM, tka), lambda b, i: (b, 0, 0, 0), pipeline_mode=per_batch),
            pl.BlockSpec((1, nka, tka, HEAD_DIM), lambda b, i: (b, 0, 0, 0), pipeline_mode=per_batch),
            pl.BlockSpec((LANES, 2 * LANES), lambda b, i: (0, 0)),
        ],
        out_specs=pl.BlockSpec((1, tq, ATT_WIDTH), lambda b, i: (b, i, 0)),
        scratch_shapes=[
            pltpu.VMEM((nkt, tq, tk), F32),
            pltpu.VMEM((IDX_HEADS, tq, LANES), F32),
            pltpu.VMEM((2, tq, LANES), F32),
            pltpu.VMEM((DSA_HEADS * tq, tka), BF16),
            pltpu.VMEM((DSA_HEADS, tq, LANES), F32),
            pltpu.VMEM((DSA_HEADS, tq, LANES), F32),
            pltpu.VMEM((DSA_HEADS, tq, LANES), F32),
            pltpu.VMEM((DSA_HEADS, tq, HEAD_DIM), F32),
        ],
        compiler_params=_cparams(("parallel", "arbitrary")),
    )(iq3, ikw, ik3, qst, kt, v, tri)


def _band_kernel(q_ref, k0_ref, k1_ref, k2_ref, v0_ref, v1_ref, v2_ref, bias_ref, o_ref):
    tq = TQ_BAND
    i = pl.program_id(1)
    q = (q_ref[0] * (HEAD_DIM ** -0.5)).astype(BF16)
    ks = [r[0].astype(BF16) for r in (k0_ref, k1_ref, k2_ref)]
    vs = [r[0].astype(BF16) for r in (v0_ref, v1_ref, v2_ref)]
    nt = len(ks)
    for h in range(CHK_HEADS):
        hs = slice(h * HEAD_DIM, (h + 1) * HEAD_DIM)
        parts = []
        for j in range(nt):
            s = lax.dot_general(q[:, hs], ks[j][:, hs], (((1,), (1,)), ((), ())), preferred_element_type=F32)
            s = s + bias_ref[h, :, j * tq:(j + 1) * tq]
            if j < nt - 1:
                s = jnp.where(i + j >= nt - 1, s, NEG_BIG)
            parts.append(s)
        m = functools.reduce(jnp.maximum, [jnp.max(s, axis=-1, keepdims=True) for s in parts])
        ps = [jnp.exp(s - m) for s in parts]
        l = functools.reduce(jnp.add, [jnp.sum(p, axis=-1, keepdims=True) for p in ps])
        o = functools.reduce(jnp.add, [jnp.dot(p.astype(BF16), vs[j][:, hs], preferred_element_type=F32)
                                        for j, p in enumerate(ps)])
        o_ref[0, :, hs] = (o / l).astype(BF16)


def _band_bias(rel_bias):
    tq = TQ_BAND
    nt = BAND_CHUNKS * CHUNK // tq + 1
    win = nt * tq
    band = (BAND_CHUNKS + 1) * CHUNK
    H = rel_bias.shape[0]
    n = band + CHUNK - 1
    g = jnp.concatenate([rel_bias, jnp.broadcast_to(rel_bias[:, -1:], (H, n - rel_bias.shape[1]))], axis=1)[:, ::-1]
    hankel = jnp.tile(g, (1, CHUNK + 1))[:, :CHUNK * (n + 1)].reshape(H, CHUNK, n + 1)[:, :, :band]
    w = hankel[:, ::-1, :]
    rows = [jnp.pad(w, ((0, 0), (0, 0), (a * CHUNK, win - band - a * CHUNK)), constant_values=NEG_BIG)
            for a in range(tq // CHUNK)]
    return jnp.concatenate(rows, axis=1).astype(F32)


def band_attention(proj, rel_bias):
    B, S, _ = proj.shape
    tq = TQ_BAND
    nt = BAND_CHUNKS * CHUNK // tq + 1
    bias = _band_bias(rel_bias)

    def kspec(col, j):
        return pl.BlockSpec((1, tq, ATT_WIDTH), lambda b, i: (b, jnp.maximum(i + j - (nt - 1), 0), col // ATT_WIDTH))

    return pl.pallas_call(
        _band_kernel,
        out_shape=jax.ShapeDtypeStruct((B, S, ATT_WIDTH), BF16),
        grid=(B, S // tq),
        in_specs=[pl.BlockSpec((1, tq, ATT_WIDTH), lambda b, i: (b, i, COL_CQ // ATT_WIDTH))]
                 + [kspec(COL_CK, j) for j in range(nt)] + [kspec(COL_CV, j) for j in range(nt)]
                 + [pl.BlockSpec((CHK_HEADS, tq, nt * tq), lambda b, i: (0, 0, 0))],
        out_specs=pl.BlockSpec((1, tq, ATT_WIDTH), lambda b, i: (b, i, 0)),
        compiler_params=_cparams(("parallel", "parallel")),
    )(proj, proj, proj, proj, proj, proj, proj, bias)


def _outproj_kernel(yc_ref, yd_ref, yb_ref, x_ref, w_ref, g_ref, b_ref, o_ref):
    c0, c1 = CONV_WIDTH, CONV_WIDTH + ATT_WIDTH
    mix = jnp.dot(yc_ref[...], w_ref[0:c0, :], preferred_element_type=F32)
    mix = mix + jnp.dot(yd_ref[...], w_ref[c0:c1, :], preferred_element_type=F32)
    mix = mix + jnp.dot(yb_ref[...], w_ref[c1:D_MODEL, :], preferred_element_type=F32)
    o_ref[...] = _layer_norm(ALPHA * x_ref[...] + mix, g_ref[...], b_ref[...])


def outproj_ln(yc, yd, yb, x, w, g, b):
    T = x.shape[0]
    tm = min(TM_OUT, T)
    rows = lambda width: pl.BlockSpec((tm, width), lambda i: (i, 0))
    vec = pl.BlockSpec((1, D_MODEL), lambda i: (0, 0))
    return pl.pallas_call(
        _outproj_kernel,
        out_shape=jax.ShapeDtypeStruct((T, D_MODEL), F32),
        grid=(T // tm,),
        in_specs=[rows(CONV_WIDTH), rows(ATT_WIDTH), rows(ATT_WIDTH), rows(D_MODEL),
                  pl.BlockSpec((D_MODEL, D_MODEL), lambda i: (0, 0)), vec, vec],
        out_specs=rows(D_MODEL),
        compiler_params=_cparams(("parallel",)),
    )(yc, yd, yb, x, w, g, b)


def _cross_kernel(x_ref, wq_ref, k_ref, v_ref, wo_ref, g_ref, b_ref, o_ref):
    x = x_ref[0]
    q = jnp.dot(x.astype(BF16), wq_ref[...], preferred_element_type=F32)
    q = (q * (MEM_HEAD_DIM ** -0.5)).astype(BF16)
    k = k_ref[0]
    v = v_ref[0]
    outs = []
    for h in range(MEM_HEADS):
        hs = slice(h * MEM_HEAD_DIM, (h + 1) * MEM_HEAD_DIM)
        s = lax.dot_general(q[:, hs], k[:, hs], (((1,), (1,)), ((), ())), preferred_element_type=F32)
        p = jnp.exp(s - jnp.max(s, axis=-1, keepdims=True))
        p = p / jnp.sum(p, axis=-1, keepdims=True)
        outs.append(jnp.dot(p.astype(BF16), v[:, hs], preferred_element_type=F32).astype(BF16))
    o = jnp.concatenate(outs, axis=-1)
    cross = jnp.dot(o, wo_ref[...], preferred_element_type=F32)
    o_ref[0] = _layer_norm(ALPHA * x + cross, g_ref[...], b_ref[...])


def cross_ln(x, k, v, wq, wo, g, b):
    B, S, _ = x.shape
    M = k.shape[1]
    tm = min(TM_CROSS, S)
    full = lambda shape: pl.BlockSpec(shape, lambda bb, i: (0,) * len(shape))
    return pl.pallas_call(
        _cross_kernel,
        out_shape=jax.ShapeDtypeStruct((B, S, D_MODEL), F32),
        grid=(B, S // tm),
        in_specs=[pl.BlockSpec((1, tm, D_MODEL), lambda bb, i: (bb, i, 0)),
                  full((D_MODEL, D_MODEL)),
                  pl.BlockSpec((1, M, D_MODEL), lambda bb, i: (bb, 0, 0)),
                  pl.BlockSpec((1, M, D_MODEL), lambda bb, i: (bb, 0, 0)),
                  full((D_MODEL, D_MODEL)), full((1, D_MODEL)), full((1, D_MODEL))],
        out_specs=pl.BlockSpec((1, tm, D_MODEL), lambda bb, i: (bb, i, 0)),
        compiler_params=_cparams(("parallel", "parallel")),
    )(x, wq, k, v, wo, g, b)


def _mlp_kernel(x_ref, w1_ref, w2_ref, g_ref, b_ref, o_ref, acc_ref):
    j = pl.program_id(1)

    @pl.when(j == 0)
    def _():
        acc_ref[...] = jnp.zeros(acc_ref.shape, F32)

    h = jnp.dot(x_ref[...].astype(BF16), w1_ref[...], preferred_element_type=F32)
    h = jnp.square(jnp.maximum(h, 0.0)).astype(BF16)
    acc_ref[...] += jnp.dot(h, w2_ref[...], preferred_element_type=F32)

    @pl.when(j == pl.num_programs(1) - 1)
    def _():
        o_ref[...] = _layer_norm(ALPHA * x_ref[...] + acc_ref[...], g_ref[...], b_ref[...])


def mlp_ln(x, w1, w2, g, b):
    T = x.shape[0]
    tm = min(TM_MLP, T)
    tf = TF_MLP
    vec = pl.BlockSpec((1, D_MODEL), lambda i, j: (0, 0))
    return pl.pallas_call(
        _mlp_kernel,
        out_shape=jax.ShapeDtypeStruct((T, D_MODEL), F32),
        grid=(T // tm, D_FF // tf),
        in_specs=[pl.BlockSpec((tm, D_MODEL), lambda i, j: (i, 0)),
                  pl.BlockSpec((D_MODEL, tf), lambda i, j: (0, j)),
                  pl.BlockSpec((tf, D_MODEL), lambda i, j: (j, 0)), vec, vec],
        out_specs=pl.BlockSpec((tm, D_MODEL), lambda i, j: (i, 0)),
        scratch_shapes=[pltpu.VMEM((tm, D_MODEL), F32)],
        compiler_params=_cparams(("parallel", "arbitrary")),
    )(x, w1, w2, g, b)


def _permute_w_in(w_in):
    sizes = (CONV_WIDTH, CONV_WIDTH, CONV_WIDTH, ATT_WIDTH, HEAD_DIM, HEAD_DIM,
             IDX_HEADS * IDX_DIM, IDX_DIM, IDX_HEADS, ATT_WIDTH, ATT_WIDTH, ATT_WIDTH)
    offs = np.concatenate([[0], np.cumsum(sizes)])
    seg = [w_in[:, offs[k]:offs[k + 1]] for k in range(len(sizes))]
    cb, cc, ch, dq, dk, dv, iq, ik, iw, cq, ck, cv = seg
    pad = jnp.zeros((w_in.shape[0], LANES - IDX_DIM - IDX_HEADS), w_in.dtype)
    return jnp.concatenate([cq, ck, cv, dq, cb, cc, ch, iq, dk, dv, ik, iw, pad], axis=1).astype(BF16)


@jax.jit
def kernel(x, mem, positions, w_in, conv_w, rel_bias, w_mix_out, ln1_g, ln1_b,
           w_mq, w_mkv, w_mo, ln2_g, ln2_b, w_ff1, w_ff2, ln3_g, ln3_b):
    B, S, D = x.shape
    T = B * S
    M = mem.shape[1]
    tabs_q = rope_tables(positions, HEAD_DIM, ROPE_DIM)
    tabs_i = rope_tables(positions, IDX_DIM, IDX_ROPE_DIM)
    mem2 = mem.reshape(B * M, D)
    xf = x.reshape(T, D)
    vec = lambda a: a.reshape(1, D)
    for l in range(w_in.shape[0]):
        proj = matmul(xf, _permute_w_in(w_in[l]), min(TM_PROJ, T)).reshape(B, S, PROJ_W)
        yconv, qst, kt, v, iq3, ik3, ikw = prep(proj, tabs_q, tabs_i, conv_w[l])
        ydsa = dsa_attention(qst, kt, v, iq3, ik3, ikw)
        yband = band_attention(proj, rel_bias[l])
        xf = outproj_ln(yconv.reshape(T, -1), ydsa.reshape(T, -1), yband.reshape(T, -1), xf,
                        w_mix_out[l].astype(BF16), vec(ln1_g[l]), vec(ln1_b[l]))
        kvm = matmul(mem2, w_mkv[l].astype(BF16), min(512, B * M)).reshape(B, M, 2 * D)
        xf = cross_ln(xf.reshape(B, S, D), kvm[..., :D].astype(BF16), kvm[..., D:].astype(BF16),
                      w_mq[l].astype(BF16), w_mo[l].astype(BF16), vec(ln2_g[l]), vec(ln2_b[l])).reshape(T, D)
        xf = mlp_ln(xf, w_ff1[l].astype(BF16), w_ff2[l].astype(BF16), vec(ln3_g[l]), vec(ln3_b[l]))
    return xf.reshape(B, S, D)
```
